```python
import jax, jax.numpy as jnp
from jax import lax
import numpy as np

D_MODEL = 1024
BATCH = 8
SEQ = 2048
DEPTH = 1
DEC_BATCH = 32
DEC_SEQ = 1
PAST_LEN = 8192
PAGE_SIZE = 128

N_META = 16
HA = 8
DK_A = 128
DV_A = 128
D_A = HA * DV_A
D_QKV_A = 3 * D_A
CONV_W = 4
CHUNK = 64
HB = 8
DH_B = 128
D_B = HB * DH_B
Q_BLOCK = 128
SB_SCALE = DH_B ** -0.5
D_FF = (((8 * D_MODEL + 2) // 3 + 255) // 256) * 256
D_IN = D_QKV_A + D_A + 2 * HA + 3 * D_B + 2 * D_MODEL
EPS = 1e-6
F32 = jnp.float32

kernel_name = 'hybrid_gdn_stickbreaking_decode_step'


def rmsnorm(x, w):
    xf = x.astype(F32)
    y = xf * lax.rsqrt(jnp.mean(xf * xf, axis=-1, keepdims=True) + EPS)
    return (y * w.astype(F32)).astype(x.dtype)


def l2norm(x):
    return x * lax.rsqrt(jnp.sum(x * x, axis=-1, keepdims=True) + EPS)


def mixer_inputs(h, norm_w, w_in):
    p = rmsnorm(h, norm_w) @ w_in
    bounds = np.cumsum([D_QKV_A, D_A, HA, HA, D_B, D_B, D_B, D_MODEL]).tolist()
    return jnp.split(p, bounds, axis=-1)


def causal_conv(buf, u, w):
    t_len = u.shape[1]
    up = jnp.concatenate([buf.astype(u.dtype), u], axis=1)
    y = sum(w[i] * up[:, i:i + t_len] for i in range(CONV_W))
    return jax.nn.silu(y), up[:, up.shape[1] - (CONV_W - 1):]


def gdn_prepare(buf, qkv, a, b, conv_w, a_log, dt_bias):
    qkv_c, new_buf = causal_conv(buf, qkv, conv_w)
    bn, t_len = qkv.shape[:2]

    def heads(t):
        return jnp.transpose(t.reshape(bn, t_len, HA, -1).astype(F32), (0, 2, 1, 3))

    q, k, v = (heads(t) for t in jnp.split(qkv_c, 3, axis=-1))
    q = l2norm(q) * (DK_A ** -0.5)
    k = l2norm(k)
    g = -jnp.exp(a_log.astype(F32)) * jax.nn.softplus(a.astype(F32) + dt_bias.astype(F32))
    beta = jax.nn.sigmoid(b.astype(F32))
    return q, k, v, jnp.transpose(g, (0, 2, 1)), jnp.transpose(beta, (0, 2, 1)), new_buf


def gdn_chunk(S, xs):
    q, k, v, g, beta = xs
    c = q.shape[2]
    incl = jnp.tril(jnp.ones((c, c), dtype=bool))
    strict = jnp.tril(jnp.ones((c, c), dtype=bool), -1)
    G = jnp.cumsum(g, axis=-1)
    decay = jnp.exp(jnp.where(incl, G[..., :, None] - G[..., None, :], -jnp.inf))
    a_mat = jnp.where(strict, beta[..., :, None] * jnp.einsum('bhid,bhjd->bhij', k, k) * decay, 0.0)
    lmat = a_mat + jnp.eye(c, dtype=F32)
    rhs = jnp.concatenate([beta[..., None] * v, (beta * jnp.exp(G))[..., None] * k], axis=-1)
    sol = lax.linalg.triangular_solve(lmat, rhs, left_side=True, lower=True, unit_diagonal=True)
    u, w = sol[..., :DV_A], sol[..., DV_A:]
    v_new = u - jnp.einsum('bhck,bhkv->bhcv', w, S)
    qk = jnp.einsum('bhid,bhjd->bhij', q, k) * decay
    o = jnp.einsum('bhck,bhkv->bhcv', q * jnp.exp(G)[..., None], S) + jnp.einsum('bhij,bhjv->bhiv', qk, v_new)
    g_last = G[..., -1]
    S_new = jnp.exp(g_last)[..., None, None] * S + jnp.einsum(
        'bhck,bhcv->bhkv', k * jnp.exp(g_last[..., None] - G)[..., None], v_new)
    return S_new, o


def gdn_scan(S0, q, k, v, g, beta, chunk):
    b, h, t_len = g.shape
    n = t_len // chunk

    def to_chunks(t):
        return jnp.moveaxis(t.reshape((b, h, n, chunk) + t.shape[3:]), 2, 0)

    S, o = lax.scan(gdn_chunk, S0, tuple(to_chunks(t) for t in (q, k, v, g, beta)))
    return S, jnp.moveaxis(o, 0, 2).reshape(b, h, t_len, DV_A)


def gdn_finish(o, z, norm_w):
    o = jnp.transpose(o, (0, 2, 1, 3))
    o = o * lax.rsqrt(jnp.mean(o * o, axis=-1, keepdims=True) + EPS) * norm_w.astype(F32)
    zf = z.reshape(o.shape).astype(F32)
    return (o * jax.nn.silu(zf)).reshape(o.shape[0], o.shape[1], D_A).astype(z.dtype)


def sb_weights(z, valid):
    log_stay = jnp.where(valid, jax.nn.log_sigmoid(-z), 0.0)
    log_after = lax.cumsum(log_stay, axis=z.ndim - 1, reverse=True) - log_stay
    return jnp.where(valid, jnp.exp(jax.nn.log_sigmoid(z) + log_after), 0.0)


def sb_logits(q, k, bias):
    return (jnp.einsum('bqhd,bkhd->bhqk', q, k).astype(F32) * SB_SCALE
            + bias.astype(F32)[None, :, None, None])


def sb_attend(qb, qpos, k, v, kpos, bias):
    z = sb_logits(qb, k, bias)
    a = sb_weights(z, kpos[None, :] < qpos[:, None])
    return jnp.einsum('bhqk,bkhd->bqhd', a.astype(v.dtype), v)


def sb_prompt(q, k, v, bias):
    b, l_len = q.shape[:2]
    seq = l_len - N_META
    kpos = jnp.arange(l_len)
    o_meta = sb_attend(q[:, :N_META], kpos[:N_META], k[:, :N_META], v[:, :N_META], kpos[:N_META], bias)
    n_blk = seq // Q_BLOCK
    qr = jnp.moveaxis(q[:, N_META:].reshape(b, n_blk, Q_BLOCK, HB, DH_B), 1, 0)
    qpos = (N_META + jnp.arange(seq)).reshape(n_blk, Q_BLOCK)
    o_real = lax.map(lambda xs: sb_attend(xs[0], xs[1], k, v, kpos, bias), (qr, qpos))
    o_real = jnp.moveaxis(o_real, 0, 1).reshape(b, seq, HB, DH_B)
    return jnp.concatenate([o_meta, o_real], axis=1)


def sb_sample(q, k_new, v_new, cache_k, cache_v, page_table, bias):
    b, t_len = q.shape[:2]
    past = page_table.shape[1] * PAGE_SIZE
    kp = cache_k[page_table].reshape(b, past, HB, DH_B).astype(q.dtype)
    vp = cache_v[page_table].reshape(b, past, HB, DH_B).astype(q.dtype)
    z = jnp.concatenate([sb_logits(q, kp, bias), sb_logits(q, k_new, bias)], axis=-1)
    qpos = past + jnp.arange(t_len)
    kpos = jnp.arange(past + t_len)
    a = sb_weights(z, kpos[None, :] < qpos[:, None]).astype(q.dtype)
    return (jnp.einsum('bhqk,bkhd->bqhd', a[..., :past], vp)
            + jnp.einsum('bhqk,bkhd->bqhd', a[..., past:], v_new))


def merge_out(o_a, o_b, gate_a, gate_b, w_pa, w_pb, w_o):
    m = jax.nn.sigmoid(gate_a) * (o_a @ w_pa) + jax.nn.sigmoid(gate_b) * (o_b @ w_pb)
    return m @ w_o


def ffn_block(h, norm_w, w_in, w_out):
    gt, up = jnp.split(rmsnorm(h, norm_w) @ w_in, 2, axis=-1)
    return h + (jax.nn.silu(gt) * up) @ w_out


def setup_inputs(seed: int = 0) -> dict:
    key = jax.random.key(seed)
    ks = jax.random.split(key, 24)
    n_pages = PAST_LEN // PAGE_SIZE
    n_phys = (DEC_BATCH * n_pages * 5 + 3) // 4

    def nrm(k, shape, s):
        return jax.random.normal(k, shape, F32) * s

    page_table = jax.random.permutation(ks[6], n_phys)[:DEC_BATCH * n_pages].reshape(
        DEC_BATCH, n_pages).astype(jnp.int32)
    return {
        'x_prompt': nrm(ks[0], (BATCH, SEQ, D_MODEL), 1.0),
        'x_sample': nrm(ks[1], (DEC_BATCH, DEC_SEQ, D_MODEL), 1.0),
        'cache_k': nrm(ks[2], (DEPTH, n_phys, PAGE_SIZE, HB, DH_B), 1.0),
        'cache_v': nrm(ks[3], (DEPTH, n_phys, PAGE_SIZE, HB, DH_B), 1.0),
        'state_rec': nrm(ks[4], (DEPTH, DEC_BATCH, HA, DK_A, DV_A), 0.1),
        'state_conv': nrm(ks[5], (DEPTH, DEC_BATCH, CONV_W - 1, D_QKV_A), 1.0),
        'page_table': page_table,
        'meta_tokens': nrm(ks[7], (N_META, D_MODEL), 1.0),
        'norm_mix': 1.0 + nrm(ks[8], (DEPTH, D_MODEL), 0.01),
        'w_in': nrm(ks[9], (DEPTH, D_MODEL, D_IN), D_MODEL ** -0.5),
        'conv_w': nrm(ks[10], (DEPTH, CONV_W, D_QKV_A), CONV_W ** -0.5),
        'a_log': jnp.log(jax.random.uniform(ks[11], (DEPTH, HA), F32, 1.0, 16.0)),
        'dt_bias': -4.0 + nrm(ks[12], (DEPTH, HA), 0.5),
        'gdn_norm': 1.0 + nrm(ks[13], (DEPTH, DV_A), 0.01),
        'sb_bias': -6.0 + nrm(ks[21], (DEPTH, HB), 0.5),
        'w_pa': nrm(ks[14], (DEPTH, D_A, D_MODEL), D_A ** -0.5),
        'w_pb': nrm(ks[15], (DEPTH, D_B, D_MODEL), D_B ** -0.5),
        'w_o': nrm(ks[16], (DEPTH, D_MODEL, D_MODEL), D_MODEL ** -0.5),
        'norm_ffn': 1.0 + nrm(ks[17], (DEPTH, D_MODEL), 0.01),
        'w_ffn_in': nrm(ks[18], (DEPTH, D_MODEL, 2 * D_FF), D_MODEL ** -0.5),
        'w_ffn_out': nrm(ks[19], (DEPTH, D_FF, D_MODEL), D_FF ** -0.5),
        'norm_final': 1.0 + nrm(ks[20], (D_MODEL,), 0.01),
    }


def reference(x_prompt, x_sample, cache_k, cache_v, state_rec, state_conv, page_table, meta_tokens,
              norm_mix, w_in, conv_w, a_log, dt_bias, gdn_norm, sb_bias, w_pa, w_pb, w_o,
              norm_ffn, w_ffn_in, w_ffn_out, norm_final):
    b, seq = x_prompt.shape[:2]
    bd, t_dec = x_sample.shape[:2]
    l_len = N_META + seq
    hp = jnp.concatenate([jnp.broadcast_to(meta_tokens.astype(x_prompt.dtype)[None], (b, N_META, D_MODEL)),
                          x_prompt], axis=1)
    hs = x_sample
    k_p, v_p, k_s, v_s, rec_p, rec_s, conv_p, conv_s = ([] for _ in range(8))
    for l in range(DEPTH):
        qkv_a, z_a, a_a, b_a, q_b, k_b, v_b, g_a, g_b = mixer_inputs(hp, norm_mix[l], w_in[l])
        q, k, v, g, beta, buf = gdn_prepare(jnp.zeros((b, CONV_W - 1, D_QKV_A), hp.dtype),
                                            qkv_a, a_a, b_a, conv_w[l], a_log[l], dt_bias[l])
        S, o_meta = gdn_scan(jnp.zeros((b, HA, DK_A, DV_A), F32),
                             *(t[:, :, :N_META] for t in (q, k, v, g, beta)), chunk=N_META)
        S, o_real = gdn_scan(S, *(t[:, :, N_META:] for t in (q, k, v, g, beta)), chunk=CHUNK)
        o_a = gdn_finish(jnp.concatenate([o_meta, o_real], axis=2), z_a, gdn_norm[l])
        qh, kh, vh = (t.reshape(b, l_len, HB, DH_B) for t in (q_b, k_b, v_b))
        o_b = sb_prompt(qh, kh, vh, sb_bias[l]).reshape(b, l_len, D_B)
        hp = hp + merge_out(o_a, o_b, g_a, g_b, w_pa[l], w_pb[l], w_o[l])
        hp = ffn_block(hp, norm_ffn[l], w_ffn_in[l], w_ffn_out[l])
        k_p.append(kh.astype(cache_k.dtype))
        v_p.append(vh.astype(cache_v.dtype))
        rec_p.append(S.astype(state_rec.dtype))
        conv_p.append(buf.astype(state_conv.dtype))

        qkv_a, z_a, a_a, b_a, q_b, k_b, v_b, g_a, g_b = mixer_inputs(hs, norm_mix[l], w_in[l])
        q, k, v, g, beta, buf = gdn_prepare(state_conv[l], qkv_a, a_a, b_a, conv_w[l], a_log[l], dt_bias[l])
        S, o_s = gdn_scan(state_rec[l].astype(F32), q, k, v, g, beta, chunk=t_dec)
        o_a = gdn_finish(o_s, z_a, gdn_norm[l])
        qh, kh, vh = (t.reshape(bd, t_dec, HB, DH_B) for t in (q_b, k_b, v_b))
        o_b = sb_sample(qh, kh, vh, cache_k[l], cache_v[l], page_table, sb_bias[l]).reshape(bd, t_dec, D_B)
        hs = hs + merge_out(o_a, o_b, g_a, g_b, w_pa[l], w_pb[l], w_o[l])
        hs = ffn_block(hs, norm_ffn[l], w_ffn_in[l], w_ffn_out[l])
        k_s.append(kh.astype(cache_k.dtype))
        v_s.append(vh.astype(cache_v.dtype))
        rec_s.append(S.astype(state_rec.dtype))
        conv_s.append(buf.astype(state_conv.dtype))

    y_prompt = rmsnorm(hp[:, N_META:], norm_final)
    y_sample = rmsnorm(hs, norm_final)
    return (y_prompt, y_sample, jnp.stack(k_p), jnp.stack(v_p), jnp.stack(k_s), jnp.stack(v_s),
            jnp.stack(rec_p), jnp.stack(rec_s), jnp.stack(conv_p), jnp.stack(conv_s))
```

```python
import functools
import math

import jax
import jax.numpy as jnp
from jax import lax
from jax.experimental import pallas as pl
from jax.experimental.pallas import tpu as pltpu

F32 = jnp.float32
BF16 = jnp.bfloat16
EPS = 1e-6
LANES = 128
SUBLANES = 8
GDN_CHUNK = 64
VMEM_LIMIT = 56 * 1024 * 1024


def _cparams(sem):
    return pltpu.CompilerParams(dimension_semantics=sem, vmem_limit_bytes=VMEM_LIMIT)


def _bdot(a, b):
    return jnp.dot(a.astype(BF16), b.astype(BF16), preferred_element_type=F32)


def _bdot_nt(a, b):
    return lax.dot_general(a.astype(BF16), b.astype(BF16), (((1,), (1,)), ((), ())),
                           preferred_element_type=F32)


def _bdot_tn(a, b):
    return lax.dot_general(a.astype(BF16), b.astype(BF16), (((0,), (0,)), ((), ())),
                           preferred_element_type=F32)


def _split2(x):
    hi = x.astype(BF16)
    lo = (x - hi.astype(F32)).astype(BF16)
    return hi, lo


def _split3(x):
    hi = x.astype(BF16)
    r = x - hi.astype(F32)
    mid = r.astype(BF16)
    lo = (r - mid.astype(F32)).astype(BF16)
    return hi, mid, lo


def _dot_exact_lhs(lhs_bf, x, passes=3):
    parts = _split3(x) if passes == 3 else _split2(x)
    out = jnp.dot(lhs_bf, parts[0], preferred_element_type=F32)
    for p in parts[1:]:
        out = out + jnp.dot(lhs_bf, p, preferred_element_type=F32)
    return out


def _dot_exact_rhs(x, rhs_bf, passes=3):
    parts = _split3(x) if passes == 3 else _split2(x)
    out = jnp.dot(parts[0], rhs_bf, preferred_element_type=F32)
    for p in parts[1:]:
        out = out + jnp.dot(p, rhs_bf, preferred_element_type=F32)
    return out


def _dot_hi(a, b):
    ah, al = _split2(a)
    bh, bl = _split2(b)
    out = jnp.dot(ah, bh, preferred_element_type=F32)
    out = out + jnp.dot(ah, bl, preferred_element_type=F32)
    return out + jnp.dot(al, bh, preferred_element_type=F32)


def _softplus(x):
    return jnp.maximum(x, 0.0) + jnp.log1p(jnp.exp(-jnp.abs(x)))


def _sigmoid(x):
    return 1.0 / (1.0 + jnp.exp(-x))


def _silu(x):
    return x * _sigmoid(x)


def _rms(x, w):
    return x * lax.rsqrt(jnp.mean(x * x, axis=-1, keepdims=True) + EPS) * w


def _head_select(width, head_dim, lane_offset=0):
    shift = head_dim.bit_length() - 1
    assert 1 << shift == head_dim
    row = lax.broadcasted_iota(jnp.int32, (LANES, width), 0)
    col_head = lax.shift_right_logical(lax.broadcasted_iota(jnp.int32, (LANES, width), 1), shift)
    return jnp.where(row == col_head + lane_offset, 1.0, 0.0).astype(BF16)


def _inproj_kernel(x_ref, nw_ref, w_ref, wab_ref, out_ref, ab_ref, xn_ref):
    @pl.when(pl.program_id(1) == 0)
    def _():
        xn_ref[...] = _rms(x_ref[...], nw_ref[...]).astype(BF16)
        ab_ref[...] = jnp.dot(xn_ref[...], wab_ref[...], preferred_element_type=F32)

    out_ref[0] = jnp.dot(xn_ref[...], w_ref[...], preferred_element_type=F32)


def _inproj(x, norm_w, w_main, w_ab, tm):
    m, d = x.shape
    n = w_main.shape[1]
    tn = d
    nj = n // tn
    return pl.pallas_call(
        _inproj_kernel,
        out_shape=(jax.ShapeDtypeStruct((nj, m, tn), F32), jax.ShapeDtypeStruct((m, LANES), F32)),
        grid=(m // tm, nj),
        in_specs=[
            pl.BlockSpec((tm, d), lambda i, j: (i, 0)),
            pl.BlockSpec((1, d), lambda i, j: (0, 0)),
            pl.BlockSpec((d, tn), lambda i, j: (0, j)),
            pl.BlockSpec((d, LANES), lambda i, j: (0, 0)),
        ],
        out_specs=(
            pl.BlockSpec((1, tm, tn), lambda i, j: (j, i, 0)),
            pl.BlockSpec((tm, LANES), lambda i, j: (i, 0)),
        ),
        scratch_shapes=[pltpu.VMEM((tm, d), BF16)],
        compiler_params=_cparams(("parallel", "arbitrary")),
        name="inproj",
    )(x, norm_w, w_main, w_ab)


def _gdn_kernel(q_ref, k_ref, v_ref, z_ref, ab_ref, cw_ref, prm_ref, gn_ref, tail0_ref, s0_ref,
                o_ref, s_out_ref, tail_out_ref,
                xpad, qs, ks, vs, gx, bx, s_scr, *, chunk, heads, conv_w):
    tb = q_ref.shape[1]
    hd = q_ref.shape[2]
    dk = hd // heads
    t = pl.program_id(1)

    @pl.when(t == 0)
    def _():
        s_scr[...] = s0_ref[...]
        xpad[:, 0:SUBLANES, :] = tail0_ref[...]

    srcs = (q_ref, k_ref, v_ref)
    dsts = (qs, ks, vs)
    for p in range(3):
        xpad[p, SUBLANES:SUBLANES + tb, :] = srcs[p][0]
    for p in range(3):
        y = None
        for i in range(conv_w):
            term = cw_ref[i, p:p + 1, :] * xpad[p, pl.ds(SUBLANES - (conv_w - 1) + i, tb), :]
            y = term if y is None else y + term
        c = _silu(y)
        if p == 2:
            dsts[p][...] = c
        else:
            scale = dk ** -0.5 if p == 0 else 1.0
            for h in range(heads):
                ch = c[:, h * dk:(h + 1) * dk]
                inv = lax.rsqrt(jnp.sum(ch * ch, axis=-1, keepdims=True) + EPS)
                dsts[p][:, h * dk:(h + 1) * dk] = ch * (inv * scale)
    new_tail = xpad[:, tb:tb + SUBLANES, :]
    xpad[:, 0:SUBLANES, :] = new_tail
    tail_out_ref[0] = new_tail

    ab = ab_ref[...]
    g_t = -jnp.exp(prm_ref[0:1, :]) * _softplus(ab + prm_ref[1:2, :])
    b_t = _sigmoid(ab)
    gx[...] = _dot_exact_rhs(g_t, _head_select(hd, dk))
    bx[...] = _dot_exact_rhs(b_t, _head_select(hd, dk, heads))

    ri = lax.broadcasted_iota(jnp.int32, (chunk, chunk), 0)
    ci = lax.broadcasted_iota(jnp.int32, (chunk, chunk), 1)
    incl = ri >= ci
    strict = ri > ci
    l_incl = jnp.where(incl, 1.0, 0.0).astype(BF16)
    eye = jnp.where(ri == ci, 1.0, 0.0)
    n_sq = int(math.log2(chunk)) - 1
    gn = gn_ref[...]

    def chunk_body(c, carry):
        r0 = pl.multiple_of(c * chunk, chunk)
        rows = pl.ds(r0, chunk)
        qc = qs[rows, :]
        kc = ks[rows, :]
        vc = vs[rows, :]
        g = gx[rows, :]
        bt = bx[rows, :]
        zc = z_ref[0, rows, :]
        gcum = _dot_exact_lhs(l_incl, g)
        e_g = jnp.exp(gcum)
        g_last = gcum[chunk - 1:chunk, :]
        e_last = jnp.exp(g_last)
        kdec = kc * jnp.exp(g_last - gcum)
        rhs_v = bt * vc
        rhs_k = bt * e_g * kc
        qe = qc * e_g
        for h in range(heads):
            sl = slice(h * dk, (h + 1) * dk)
            sc = slice(h * dk, h * dk + chunk)
            kh = kc[:, sl]
            kk = _bdot_nt(kh, kh)
            qk = _bdot_nt(qc[:, sl], kh)
            g_row = gcum[:, sl].T[0:chunk, :]
            decay = jnp.where(incl, jnp.exp(gcum[:, sc] - g_row), 0.0)
            a = jnp.where(strict, bt[:, sc] * kk * decay, 0.0)
            tinv = eye - a
            x = a
            for _ in range(n_sq):
                x = _dot_hi(x, x)
                tinv = tinv + _dot_hi(tinv, x)
            u = _dot_hi(tinv, rhs_v[:, sl])
            w = _dot_hi(tinv, rhs_k[:, sl])
            s = s_scr[h]
            v_new = u - _bdot(w, s)
            o_h = _bdot(qe[:, sl], s) + _bdot(qk * decay, v_new)
            s_scr[h] = e_last[:, sl] * s + _bdot_tn(kdec[:, sl], v_new)
            o_n = o_h * lax.rsqrt(jnp.mean(o_h * o_h, axis=-1, keepdims=True) + EPS) * gn
            o_ref[rows, sl] = o_n * _silu(zc[:, sl])
        return carry

    lax.fori_loop(0, tb // chunk, chunk_body, 0)
    s_out_ref[0] = s_scr[...]


def _gdn(p_all, ab, conv_w3, prm, gn, tail0, s0, *, n_seq, seq, tb, chunk, row_block_off, heads):
    hd = p_all.shape[2]
    dk = hd // heads
    nblk = seq // tb
    conv_w = conv_w3.shape[0]

    def plane(p):
        return pl.BlockSpec((1, tb, hd), lambda b, t: (p, row_block_off + b * nblk + t, 0))

    kern = functools.partial(_gdn_kernel, chunk=chunk, heads=heads, conv_w=conv_w)
    return pl.pallas_call(
        kern,
        out_shape=(
            jax.ShapeDtypeStruct((n_seq * seq, hd), F32),
            jax.ShapeDtypeStruct((n_seq, heads, dk, dk), F32),
            jax.ShapeDtypeStruct((n_seq, 3, SUBLANES, hd), F32),
        ),
        grid=(n_seq, nblk),
        in_specs=[
            plane(0), plane(1), plane(2), plane(3),
            pl.BlockSpec((tb, LANES), lambda b, t: (row_block_off + b * nblk + t, 0)),
            pl.BlockSpec((conv_w, 3, hd), lambda b, t: (0, 0, 0)),
            pl.BlockSpec((SUBLANES, LANES), lambda b, t: (0, 0)),
            pl.BlockSpec((1, dk), lambda b, t: (0, 0)),
            pl.BlockSpec((3, SUBLANES, hd), lambda b, t: (0, 0, 0)),
            pl.BlockSpec((heads, dk, dk), lambda b, t: (0, 0, 0)),
        ],
        out_specs=(
            pl.BlockSpec((tb, hd), lambda b, t: (b * nblk + t, 0)),
            pl.BlockSpec((1, heads, dk, dk), lambda b, t: (b, 0, 0, 0)),
            pl.BlockSpec((1, 3, SUBLANES, hd), lambda b, t: (b, 0, 0, 0)),
        ),
        scratch_shapes=[
            pltpu.VMEM((3, tb + SUBLANES, hd), F32),
            pltpu.VMEM((tb, hd), F32), pltpu.VMEM((tb, hd), F32), pltpu.VMEM((tb, hd), F32),
            pltpu.VMEM((tb, hd), F32), pltpu.VMEM((tb, hd), F32),
            pltpu.VMEM((heads, dk, dk), F32),
        ],
        compiler_params=_cparams(("parallel", "arbitrary")),
        name=f"gdn_chunk{chunk}",
    )(p_all, p_all, p_all, p_all, ab, conv_w3, prm, gn, tail0, s0)


def _sb_prompt_kernel(bias_ref, q_ref, k_ref, v_ref, km_ref, vm_ref, o_ref, kbf, vtb, *, scale, tq):
    seq = k_ref.shape[1]
    dh = k_ref.shape[2]
    n_meta = km_ref.shape[0]
    nkb = seq // tq
    h = pl.program_id(1)
    qi = pl.program_id(2)

    @pl.when(qi == 0)
    def _():
        for j in range(nkb):
            kbf[j] = k_ref[0, j * tq:(j + 1) * tq, :].astype(BF16)
            vtb[j] = v_ref[0, j * tq:(j + 1) * tq, :].T.astype(BF16)

    bias = bias_ref[h]
    qt = (q_ref[0] * scale).T.astype(BF16)

    def suffix_mat(n):
        r = lax.broadcasted_iota(jnp.int32, (n, n), 0)
        c = lax.broadcasted_iota(jnp.int32, (n, n), 1)
        return jnp.where(c > r, 1.0, 0.0).astype(BF16)

    u_full = suffix_mat(tq)

    def tile(k_bf, vt_bf, u_bf, carry, acc, mask):
        zt = jnp.dot(k_bf, qt, preferred_element_type=F32) + bias
        sp = _softplus(zt)
        ls = sp if mask is None else jnp.where(mask, sp, 0.0)
        after = _dot_exact_lhs(u_bf, ls, passes=2)
        a = jnp.exp(zt - sp - after - carry)
        if mask is not None:
            a = jnp.where(mask, a, 0.0)
        acc = acc + jnp.dot(vt_bf, a.astype(BF16), preferred_element_type=F32)
        carry = carry + after[0:1, :] + ls[0:1, :]
        return carry, acc

    r = lax.broadcasted_iota(jnp.int32, (tq, tq), 0)
    c = lax.broadcasted_iota(jnp.int32, (tq, tq), 1)
    carry0 = jnp.zeros((1, tq), F32)
    acc0 = jnp.zeros((dh, tq), F32)
    carry, acc = tile(kbf[qi], vtb[qi], u_full, carry0, acc0, r < c)

    def body(i, ca):
        kb = qi - 1 - i
        return tile(kbf[kb], vtb[kb], u_full, ca[0], ca[1], None)

    carry, acc = lax.fori_loop(0, qi, body, (carry, acc))
    carry, acc = tile(km_ref[...].astype(BF16), vm_ref[...].T.astype(BF16), suffix_mat(n_meta), carry, acc, None)
    o_ref[...] = acc.T


def _sb_prompt(p_all, k_meta, v_meta, bias, *, n_seq, seq, heads, q_plane, tq):
    hd = p_all.shape[2]
    dh = hd // heads
    nq = seq // tq
    n_meta = k_meta.shape[0]
    kern = functools.partial(_sb_prompt_kernel, scale=dh ** -0.5, tq=tq)
    return pl.pallas_call(
        kern,
        out_shape=jax.ShapeDtypeStruct((n_seq * seq, hd), F32),
        grid=(n_seq, heads, nq),
        in_specs=[
            pl.BlockSpec(memory_space=pltpu.SMEM),
            pl.BlockSpec((1, tq, dh), lambda b, h, i: (q_plane, b * nq + i, h)),
            pl.BlockSpec((1, seq, dh), lambda b, h, i: (q_plane + 1, b, h)),
            pl.BlockSpec((1, seq, dh), lambda b, h, i: (q_plane + 2, b, h)),
            pl.BlockSpec((n_meta, dh), lambda b, h, i: (0, h)),
            pl.BlockSpec((n_meta, dh), lambda b, h, i: (0, h)),
        ],
        out_specs=pl.BlockSpec((tq, dh), lambda b, h, i: (b * nq + i, h)),
        scratch_shapes=[pltpu.VMEM((nq, tq, dh), BF16), pltpu.VMEM((nq, dh, tq), BF16)],
        compiler_params=_cparams(("parallel", "parallel", "arbitrary")),
        name="sb_prompt",
    )(bias, p_all, p_all, p_all, k_meta, v_meta)


def _merge_kernel(oa_ref, ob_ref, ga_ref, gb_ref, x_ref, wpa_ref, wpb_ref, wo_ref, out_ref):
    pa = jnp.dot(oa_ref[...].astype(BF16), wpa_ref[...], preferred_element_type=F32)
    pb = jnp.dot(ob_ref[...].astype(BF16), wpb_ref[...], preferred_element_type=F32)
    m = _sigmoid(ga_ref[0]) * pa + _sigmoid(gb_ref[0]) * pb
    out_ref[...] = x_ref[...] + jnp.dot(m.astype(BF16), wo_ref[...], preferred_element_type=F32)


def _merge(o_a, o_b, p_all, x, w_pa, w_pb, w_o, *, gate_plane, tm):
    m, d = x.shape
    da = o_a.shape[1]
    db = o_b.shape[1]

    def whole(shape):
        return pl.BlockSpec(shape, lambda i: (0, 0))

    return pl.pallas_call(
        _merge_kernel,
        out_shape=jax.ShapeDtypeStruct((m, d), F32),
        grid=(m // tm,),
        in_specs=[
            pl.BlockSpec((tm, da), lambda i: (i, 0)),
            pl.BlockSpec((tm, db), lambda i: (i, 0)),
            pl.BlockSpec((1, tm, d), lambda i: (gate_plane, i, 0)),
            pl.BlockSpec((1, tm, d), lambda i: (gate_plane + 1, i, 0)),
            pl.BlockSpec((tm, d), lambda i: (i, 0)),
            whole(w_pa.shape), whole(w_pb.shape), whole(w_o.shape),
        ],
        out_specs=pl.BlockSpec((tm, d), lambda i: (i, 0)),
        compiler_params=_cparams(("parallel",)),
        name="merge",
    )(o_a, o_b, p_all, p_all, x, w_pa, w_pb, w_o)


def _ffn_kernel(h_ref, nw_ref, wg_ref, wu_ref, wo_ref, nf_ref, y_ref, xn_ref, acc_ref):
    f = pl.program_id(1)

    @pl.when(f == 0)
    def _():
        h = h_ref[...]
        xn_ref[...] = _rms(h, nw_ref[...]).astype(BF16)
        acc_ref[...] = h

    xn = xn_ref[...]
    gt = jnp.dot(xn, wg_ref[...], preferred_element_type=F32)
    up = jnp.dot(xn, wu_ref[...], preferred_element_type=F32)
    acc_ref[...] += jnp.dot((_silu(gt) * up).astype(BF16), wo_ref[...], preferred_element_type=F32)

    @pl.when(f == pl.num_programs(1) - 1)
    def _():
        y_ref[...] = _rms(acc_ref[...], nf_ref[...])


def _ffn(h, norm_w, w_in, w_out, norm_f, *, tm, tf):
    m, d = h.shape
    dff = w_out.shape[0]
    nf = dff // tf
    return pl.pallas_call(
        _ffn_kernel,
        out_shape=jax.ShapeDtypeStruct((m, d), F32),
        grid=(m // tm, nf),
        in_specs=[
            pl.BlockSpec((tm, d), lambda i, f: (i, 0)),
            pl.BlockSpec((1, d), lambda i, f: (0, 0)),
            pl.BlockSpec((d, tf), lambda i, f: (0, f)),
            pl.BlockSpec((d, tf), lambda i, f: (0, f + nf)),
            pl.BlockSpec((tf, d), lambda i, f: (f, 0)),
            pl.BlockSpec((1, d), lambda i, f: (0, 0)),
        ],
        out_specs=pl.BlockSpec((tm, d), lambda i, f: (i, 0)),
        scratch_shapes=[pltpu.VMEM((tm, d), BF16), pltpu.VMEM((tm, d), F32)],
        compiler_params=_cparams(("parallel", "arbitrary")),
        name="ffn",
    )(h, norm_w, w_in, w_in, w_out, norm_f)


def _gdn_step_kernel(q_ref, k_ref, v_ref, z_ref, ab_ref, cw_ref, prm_ref, gn_ref, buf_ref, s_ref,
                     o_ref, s_out_ref, qs, ks, vs, gx, bx, *, heads, conv_w):
    nb = qs.shape[0]
    hd = qs.shape[1]
    dk = hd // heads
    b = pl.program_id(0)

    @pl.when(b == 0)
    def _():
        srcs = (q_ref, k_ref, v_ref)
        dsts = (qs, ks, vs)
        for p in range(3):
            y = cw_ref[conv_w - 1, p:p + 1, :] * srcs[p][0, 0:nb, :]
            for i in range(conv_w - 1):
                y = y + cw_ref[i, p:p + 1, :] * buf_ref[i, p]
            c = _silu(y)
            if p == 2:
                dsts[p][...] = c
            else:
                scale = dk ** -0.5 if p == 0 else 1.0
                for h in range(heads):
                    ch = c[:, h * dk:(h + 1) * dk]
                    inv = lax.rsqrt(jnp.sum(ch * ch, axis=-1, keepdims=True) + EPS)
                    dsts[p][:, h * dk:(h + 1) * dk] = ch * (inv * scale)
        ab = ab_ref[0:nb, :]
        g_t = -jnp.exp(prm_ref[0:1, :]) * _softplus(ab + prm_ref[1:2, :])
        b_t = _sigmoid(ab)
        gx[...] = _dot_exact_rhs(g_t, _head_select(hd, dk))
        bx[...] = _dot_exact_rhs(b_t, _head_select(hd, dk, heads))

    rb = pl.ds(b, 1)
    qrow = qs[rb, :]
    krow = ks[rb, :]
    vrow = vs[rb, :]
    e_g = jnp.exp(gx[rb, :])
    beta = bx[rb, :]
    zrow = z_ref[0, rb, :]
    gn = gn_ref[...]
    first = lax.broadcasted_iota(jnp.int32, (SUBLANES, dk), 0) == 0
    for h in range(heads):
        sl = slice(h * dk, (h + 1) * dk)
        s = s_ref[0, h]
        kh = krow[:, sl]
        qh = qrow[:, sl]
        lhs = jnp.where(first, kh, qh)
        prod = _dot_hi(jnp.broadcast_to(lhs, (SUBLANES, dk)), s)
        k_s = prod[0:1, :]
        q_s = prod[1:2, :]
        v_new = beta[:, sl] * (vrow[:, sl] - e_g[:, sl] * k_s)
        qk = jnp.sum(qh * kh, axis=-1, keepdims=True)
        o_h = e_g[:, sl] * q_s + qk * v_new
        k8 = jnp.where(first, kh, 0.0)
        v8 = jnp.where(first, v_new, 0.0)
        kh_hi, kh_lo = _split2(k8)
        vn_hi, vn_lo = _split2(v8)
        outer = (_bdot_tn(kh_hi, vn_hi) + _bdot_tn(kh_hi, vn_lo)) + _bdot_tn(kh_lo, vn_hi)
        s_out_ref[0, h] = e_g[:, sl] * s + outer
        o_n = o_h * lax.rsqrt(jnp.mean(o_h * o_h, axis=-1, keepdims=True) + EPS) * gn
        o_ref[0, :, sl] = o_n * _silu(zrow[:, sl])


def _gdn_step(p_small, ab, conv_w3, prm, gn, buf, state, *, heads):
    nb = state.shape[0]
    ms, hd = p_small.shape[1], p_small.shape[2]
    dk = hd // heads
    conv_w = conv_w3.shape[0]
    kern = functools.partial(_gdn_step_kernel, heads=heads, conv_w=conv_w)

    def plane(p):
        return pl.BlockSpec((1, ms, hd), lambda b: (p, 0, 0))

    return pl.pallas_call(
        kern,
        out_shape=(jax.ShapeDtypeStruct((nb, 1, hd), F32), jax.ShapeDtypeStruct(state.shape, F32)),
        grid=(nb,),
        in_specs=[
            plane(0), plane(1), plane(2), plane(3),
            pl.BlockSpec((ms, LANES), lambda b: (0, 0)),
            pl.BlockSpec((conv_w, 3, hd), lambda b: (0, 0, 0)),
            pl.BlockSpec((SUBLANES, LANES), lambda b: (0, 0)),
            pl.BlockSpec((1, dk), lambda b: (0, 0)),
            pl.BlockSpec((conv_w - 1, 3, nb, hd), lambda b: (0, 0, 0, 0)),
            pl.BlockSpec((1, heads, dk, dk), lambda b: (b, 0, 0, 0)),
        ],
        out_specs=(
            pl.BlockSpec((1, 1, hd), lambda b: (b, 0, 0)),
            pl.BlockSpec((1, heads, dk, dk), lambda b: (b, 0, 0, 0)),
        ),
        scratch_shapes=[pltpu.VMEM((nb, hd), F32) for _ in range(5)],
        compiler_params=_cparams(("arbitrary",)),
        name="gdn_step",
    )(p_small, p_small, p_small, p_small, ab, conv_w3, prm, gn, buf, state)


def _sb_step_kernel(pt_ref, q_ref, kn_ref, vn_ref, bias_ref, *refs, scale, pages_per_step, heads, past_len):
    k_refs = refs[:pages_per_step]
    v_refs = refs[pages_per_step:2 * pages_per_step]
    o_ref = refs[2 * pages_per_step]
    qbd, acc, carry = refs[2 * pages_per_step + 1:]
    page = k_refs[0].shape[1]
    hd = k_refs[0].shape[2]
    dh = hd // heads
    b = pl.program_id(0)
    j = pl.program_id(1)
    lane_head = lax.broadcasted_iota(jnp.int32, (dh, LANES), 1)

    @pl.when(j == 0)
    def _():
        qrow = q_ref[pl.ds(b, 1), :] * scale
        for h in range(heads):
            qh_t = jnp.broadcast_to(qrow[:, h * dh:(h + 1) * dh], (LANES, dh)).T
            qbd[h * dh:(h + 1) * dh, :] = jnp.where(lane_head == h, qh_t, 0.0).astype(BF16)
        acc[...] = jnp.zeros_like(acc)
        carry[...] = jnp.zeros_like(carry)

    r = lax.broadcasted_iota(jnp.int32, (page, page), 0)
    c = lax.broadcasted_iota(jnp.int32, (page, page), 1)
    u_bf = jnp.where(c > r, 1.0, 0.0).astype(BF16)
    expand = _head_select(hd, dh)
    bias = bias_ref[...]
    for i in range(pages_per_step):
        z = jnp.dot(k_refs[i][0].astype(BF16), qbd[...], preferred_element_type=F32) + bias
        sp = _softplus(z)
        after = _dot_exact_lhs(u_bf, sp, passes=2)
        a = jnp.exp(z - sp - after - carry[...])
        a_x = jnp.dot(a.astype(BF16), expand, preferred_element_type=F32)
        acc[...] += a_x * v_refs[i][0]
        carry[...] += after[0:1, :] + sp[0:1, :]

    @pl.when(j == pl.num_programs(1) - 1)
    def _():
        out = jnp.sum(acc[...], axis=0, keepdims=True)
        qrow = q_ref[pl.ds(b, 1), :] * scale
        krow = kn_ref[pl.ds(b, 1), :]
        vrow = vn_ref[pl.ds(b, 1), :]
        self_valid = past_len < past_len
        for h in range(heads):
            sl = slice(h * dh, (h + 1) * dh)
            z_new = jnp.sum(qrow[:, sl] * krow[:, sl], axis=-1, keepdims=True) + bias[:, h:h + 1]
            a_new = jnp.where(self_valid, jnp.exp(z_new - _softplus(z_new)), 0.0)
            o_ref[0, :, sl] = out[:, sl] + a_new * vrow[:, sl]


def _sb_step(page_table, q, k_new, v_new, bias_row, cache_k, cache_v, *, heads, pages_per_step):
    nb, n_pages = page_table.shape
    page, hd = cache_k.shape[1], cache_k.shape[2]
    nsteps = n_pages // pages_per_step
    kern = functools.partial(_sb_step_kernel, scale=(hd // heads) ** -0.5, pages_per_step=pages_per_step,
                             heads=heads, past_len=n_pages * page)

    def page_spec(i):
        return pl.BlockSpec((1, page, hd), lambda b, j, pt: (pt[b, n_pages - 1 - (j * pages_per_step + i)], 0, 0))

    def whole(shape):
        return pl.BlockSpec(shape, lambda b, j, pt: (0, 0))

    grid_spec = pltpu.PrefetchScalarGridSpec(
        num_scalar_prefetch=1,
        grid=(nb, nsteps),
        in_specs=[whole(q.shape), whole(k_new.shape), whole(v_new.shape), whole(bias_row.shape)]
        + [page_spec(i) for i in range(pages_per_step)] * 2,
        out_specs=pl.BlockSpec((1, 1, hd), lambda b, j, pt: (b, 0, 0)),
        scratch_shapes=[pltpu.VMEM((hd, LANES), BF16), pltpu.VMEM((page, hd), F32), pltpu.VMEM((1, LANES), F32)],
    )
    out = pl.pallas_call(
        kern,
        out_shape=jax.ShapeDtypeStruct((nb, 1, hd), F32),
        grid_spec=grid_spec,
        compiler_params=_cparams(("parallel", "arbitrary")),
        name="sb_step",
    )(page_table, q, k_new, v_new, bias_row, *([cache_k] * pages_per_step), *([cache_v] * pages_per_step))
    return out.reshape(nb, hd)


def _layer(l, x_prompt, x_sample, cache_k, cache_v, state_rec, state_conv, page_table, meta_tokens,
           norm_mix, w_in, conv_w, a_log, dt_bias, gdn_norm, sb_bias, w_pa, w_pb, w_o,
           norm_ffn, w_ffn_in, w_ffn_out, norm_final):
    n_seq, seq, d = x_prompt.shape
    nb = x_sample.shape[0]
    n_meta = meta_tokens.shape[0]
    heads_a = a_log.shape[1]
    heads_b = sb_bias.shape[1]
    d_qkv = conv_w.shape[2]
    d_a = d_qkv // 3
    d_b = w_pb.shape[1]
    conv_width = conv_w.shape[1]
    dk = d_a // heads_a
    assert d_a == d and d_b == d and 2 * heads_a <= LANES

    w = w_in[l]
    c_ab = d_qkv + d_a
    w_main = jnp.concatenate([w[:, :c_ab], w[:, c_ab + 2 * heads_a:]], axis=1).astype(BF16)
    w_ab = jnp.pad(w[:, c_ab:c_ab + 2 * heads_a], ((0, 0), (0, LANES - 2 * heads_a))).astype(BF16)
    nm = norm_mix[l].reshape(1, d)
    conv_w3 = conv_w[l].reshape(conv_width, 3, d_a)
    prm = jnp.zeros((SUBLANES, LANES), F32).at[0, :heads_a].set(a_log[l]).at[1, :heads_a].set(dt_bias[l])
    gn = gdn_norm[l].reshape(1, dk)
    bias = sb_bias[l]
    wpa, wpb, wo = w_pa[l].astype(BF16), w_pb[l].astype(BF16), w_o[l].astype(BF16)
    wfi, wfo = w_ffn_in[l].astype(BF16), w_ffn_out[l].astype(BF16)
    nffn = norm_ffn[l].reshape(1, d)
    nfin = norm_final.reshape(1, d)

    ms = -(-(nb + n_meta) // 64) * 64
    x_small = jnp.concatenate([x_sample.reshape(nb, d), meta_tokens.astype(F32),
                               jnp.zeros((ms - nb - n_meta, d), F32)], axis=0)
    p_small, ab_small = _inproj(x_small, nm, w_main, w_ab, tm=ms)

    zeros_tail = jnp.zeros((3, SUBLANES, d_a), F32)
    zeros_state = jnp.zeros((heads_a, dk, dk), F32)
    _, s_meta, tail_meta = _gdn(p_small, ab_small, conv_w3, prm, gn, zeros_tail, zeros_state,
                                n_seq=1, seq=n_meta, tb=n_meta, chunk=n_meta,
                                row_block_off=nb // n_meta, heads=heads_a)
    k_meta = p_small[5, nb:nb + n_meta]
    v_meta = p_small[6, nb:nb + n_meta]

    xp = x_prompt.reshape(n_seq * seq, d)
    p_all, ab_all = _inproj(xp, nm, w_main, w_ab, tm=1024)
    o_a, s_prompt, tail_prompt = _gdn(p_all, ab_all, conv_w3, prm, gn, tail_meta[0], s_meta[0],
                                      n_seq=n_seq, seq=seq, tb=256, chunk=GDN_CHUNK,
                                      row_block_off=0, heads=heads_a)
    o_b = _sb_prompt(p_all, k_meta, v_meta, bias, n_seq=n_seq, seq=seq, heads=heads_b, q_plane=4, tq=256)
    h1 = _merge(o_a, o_b, p_all, xp, wpa, wpb, wo, gate_plane=7, tm=512)
    y_prompt = _ffn(h1, nffn, wfi, wfo, nfin, tm=512, tf=1408).reshape(n_seq, seq, d)

    dh = d_b // heads_b
    k_prompt = jnp.concatenate([jnp.broadcast_to(k_meta[None], (n_seq, n_meta, d_b)),
                                p_all[5].reshape(n_seq, seq, d_b)], axis=1).reshape(n_seq, n_meta + seq, heads_b, dh)
    v_prompt = jnp.concatenate([jnp.broadcast_to(v_meta[None], (n_seq, n_meta, d_b)),
                                p_all[6].reshape(n_seq, seq, d_b)], axis=1).reshape(n_seq, n_meta + seq, heads_b, dh)
    conv_prompt = jnp.transpose(tail_prompt[:, :, SUBLANES - (conv_width - 1):, :], (0, 2, 1, 3)).reshape(
        n_seq, conv_width - 1, d_qkv)

    buf = jnp.transpose(state_conv[l].reshape(nb, conv_width - 1, 3, d_a), (1, 2, 0, 3))
    o_a_s, s_sample = _gdn_step(p_small, ab_small, conv_w3, prm, gn, buf, state_rec[l], heads=heads_a)
    o_a_s = o_a_s.reshape(nb, d_a)
    n_phys, page = cache_k.shape[1], cache_k.shape[2]
    bias_row = jnp.zeros((1, LANES), F32).at[0, :heads_b].set(bias)
    o_b_s = _sb_step(page_table, p_small[4, :nb], p_small[5, :nb], p_small[6, :nb], bias_row,
                     cache_k[l].reshape(n_phys, page, d_b), cache_v[l].reshape(n_phys, page, d_b),
                     heads=heads_b, pages_per_step=4)
    h1_s = _merge(o_a_s, o_b_s, p_small, x_small[:nb], wpa, wpb, wo, gate_plane=7, tm=nb)
    y_sample = _ffn(h1_s, nffn, wfi, wfo, nfin, tm=nb, tf=1408).reshape(nb, 1, d)

    k_sample = p_small[5, :nb].reshape(nb, 1, heads_b, dh)
    v_sample = p_small[6, :nb].reshape(nb, 1, heads_b, dh)
    qkv_new = jnp.transpose(p_small[0:3, :nb], (1, 0, 2)).reshape(nb, 1, d_qkv)
    conv_sample = jnp.concatenate([state_conv[l][:, 1:], qkv_new], axis=1)
    return (y_prompt, y_sample, k_prompt, v_prompt, k_sample, v_sample,
            s_prompt, s_sample, conv_prompt, conv_sample)


def kernel(x_prompt, x_sample, cache_k, cache_v, state_rec, state_conv, page_table, meta_tokens, norm_mix, w_in, conv_w, a_log, dt_bias, gdn_norm, sb_bias, w_pa, w_pb, w_o, norm_ffn, w_ffn_in, w_ffn_out, norm_final):
    depth = w_in.shape[0]
    assert depth == 1 and x_sample.shape[1] == 1, "single layer, single decode token per sequence"
    outs = _layer(0, x_prompt, x_sample, cache_k, cache_v, state_rec, state_conv, page_table, meta_tokens,
                  norm_mix, w_in, conv_w, a_log, dt_bias, gdn_norm, sb_bias, w_pa, w_pb, w_o,
                  norm_ffn, w_ffn_in, w_ffn_out, norm_final)
    (y_prompt, y_sample, k_p, v_p, k_s, v_s, rec_p, rec_s, conv_p, conv_s) = outs
    return (y_prompt, y_sample, k_p[None], v_p[None], k_s[None], v_s[None],
            rec_p[None], rec_s[None], conv_p[None], conv_s[None])
```

```python
import functools

import jax
import jax.numpy as jnp
from jax import lax
from jax.experimental import pallas as pl
from jax.experimental.pallas import tpu as pltpu

F32 = jnp.float32
BF16 = jnp.bfloat16
EPS = 1e-6
LANES = 128
SUBLANES = 8
MXU_DIM = 256
GDN_CHUNK = 64
VMEM_LIMIT = 56 * 1024 * 1024


def _cparams(sem):
    return pltpu.CompilerParams(dimension_semantics=sem, vmem_limit_bytes=VMEM_LIMIT)


def _bdot(a, b):
    return jnp.dot(a.astype(BF16), b.astype(BF16), preferred_element_type=F32)


def _bdot_nt(a, b):
    return lax.dot_general(a.astype(BF16), b.astype(BF16), (((1,), (1,)), ((), ())),
                           preferred_element_type=F32)


def _bdot_tn(a, b):
    return lax.dot_general(a.astype(BF16), b.astype(BF16), (((0,), (0,)), ((), ())),
                           preferred_element_type=F32)


def _split2(x):
    hi = x.astype(BF16)
    lo = (x - hi.astype(F32)).astype(BF16)
    return hi, lo


def _split3(x):
    hi = x.astype(BF16)
    r = x - hi.astype(F32)
    mid = r.astype(BF16)
    lo = (r - mid.astype(F32)).astype(BF16)
    return hi, mid, lo


def _dot_exact_lhs(lhs_bf, x, passes=3):
    parts = _split3(x) if passes == 3 else _split2(x)
    out = jnp.dot(lhs_bf, parts[0], preferred_element_type=F32)
    for p in parts[1:]:
        out = out + jnp.dot(lhs_bf, p, preferred_element_type=F32)
    return out


def _dot_exact_rhs(x, rhs_bf, passes=3):
    parts = _split3(x) if passes == 3 else _split2(x)
    out = jnp.dot(parts[0], rhs_bf, preferred_element_type=F32)
    for p in parts[1:]:
        out = out + jnp.dot(p, rhs_bf, preferred_element_type=F32)
    return out


def _dot_hi(a, b):
    ah, al = _split2(a)
    bh, bl = _split2(b)
    out = jnp.dot(ah, bh, preferred_element_type=F32)
    out = out + jnp.dot(ah, bl, preferred_element_type=F32)
    return out + jnp.dot(al, bh, preferred_element_type=F32)


def _softplus(x):
    return jnp.maximum(x, 0.0) + jnp.log1p(jnp.exp(-jnp.abs(x)))


def _sigmoid(x):
    return 1.0 / (1.0 + jnp.exp(-x))


def _silu(x):
    return x * _sigmoid(x)


def _rms(x, w):
    return x * lax.rsqrt(jnp.mean(x * x, axis=-1, keepdims=True) + EPS) * w


def _head_select(width, head_dim, lane_offset=0):
    shift = head_dim.bit_length() - 1
    assert 1 << shift == head_dim
    row = lax.broadcasted_iota(jnp.int32, (LANES, width), 0)
    col_head = lax.shift_right_logical(lax.broadcasted_iota(jnp.int32, (LANES, width), 1), shift)
    return jnp.where(row == col_head + lane_offset, 1.0, 0.0).astype(BF16)


def _inproj_kernel(x_ref, nw_ref, w_ref, wab_ref, out_ref, ab_ref, xn_ref):
    @pl.when(pl.program_id(1) == 0)
    def _():
        xn_ref[...] = _rms(x_ref[...], nw_ref[...]).astype(BF16)
        ab_ref[...] = jnp.dot(xn_ref[...], wab_ref[...], preferred_element_type=F32)

    out_ref[0] = jnp.dot(xn_ref[...], w_ref[...], preferred_element_type=F32)


def _inproj(x, norm_w, w_main, w_ab, tm):
    m, d = x.shape
    n = w_main.shape[1]
    tn = d
    nj = n // tn
    assert m % tm == 0 and n % tn == 0
    return pl.pallas_call(
        _inproj_kernel,
        out_shape=(jax.ShapeDtypeStruct((nj, m, tn), F32), jax.ShapeDtypeStruct((m, LANES), F32)),
        grid=(m // tm, nj),
        in_specs=[
            pl.BlockSpec((tm, d), lambda i, j: (i, 0)),
            pl.BlockSpec((1, d), lambda i, j: (0, 0)),
            pl.BlockSpec((d, tn), lambda i, j: (0, j)),
            pl.BlockSpec((d, LANES), lambda i, j: (0, 0)),
        ],
        out_specs=(
            pl.BlockSpec((1, tm, tn), lambda i, j: (j, i, 0)),
            pl.BlockSpec((tm, LANES), lambda i, j: (i, 0)),
        ),
        scratch_shapes=[pltpu.VMEM((tm, d), BF16)],
        compiler_params=_cparams(("parallel", "arbitrary")),
        name="inproj",
    )(x, norm_w, w_main, w_ab)


def _gdn_kernel(q_ref, k_ref, v_ref, z_ref, ab_ref, cw_ref, prm_ref, gn_ref, tail0_ref, s0_ref,
                o_ref, s_out_ref, tail_out_ref,
                xpad, qs, ks, vs, gx, bx, gpk, bpk, s_scr, *, chunk, heads, conv_w):
    tb = q_ref.shape[1]
    hd = q_ref.shape[2]
    dk = hd // heads
    t = pl.program_id(1)

    @pl.when(t == 0)
    def _():
        s_scr[...] = s0_ref[...]
        xpad[:, 0:SUBLANES, :] = tail0_ref[...]

    srcs = (q_ref, k_ref, v_ref)
    dsts = (qs, ks, vs)
    for p in range(3):
        xpad[p, SUBLANES:SUBLANES + tb, :] = srcs[p][0]
    for p in range(3):
        y = None
        for i in range(conv_w):
            term = cw_ref[i, p:p + 1, :] * xpad[p, pl.ds(SUBLANES - (conv_w - 1) + i, tb), :]
            y = term if y is None else y + term
        c = _silu(y)
        if p == 2:
            dsts[p][...] = c
        else:
            scale = dk ** -0.5 if p == 0 else 1.0
            for h in range(heads):
                ch = c[:, h * dk:(h + 1) * dk]
                inv = lax.rsqrt(jnp.sum(ch * ch, axis=-1, keepdims=True) + EPS)
                dsts[p][:, h * dk:(h + 1) * dk] = ch * (inv * scale)
    new_tail = xpad[:, tb:tb + SUBLANES, :]
    xpad[:, 0:SUBLANES, :] = new_tail
    tail_out_ref[0] = new_tail

    ab = ab_ref[...]
    g_t = -jnp.exp(prm_ref[0:1, :]) * _softplus(ab + prm_ref[1:2, :])
    b_t = _sigmoid(ab)
    gx[...] = _dot_exact_rhs(g_t, _head_select(hd, dk))
    bx[...] = _dot_exact_rhs(b_t, _head_select(hd, dk, heads))
    pk = heads * chunk
    gpk[...] = _dot_exact_rhs(g_t, _head_select(pk, chunk))
    bpk[...] = _dot_exact_rhs(b_t, _head_select(pk, chunk, heads))

    hpg = min(heads, MXU_DIM // chunk)
    ngrp = heads // hpg
    pw = hpg * chunk
    gw = hpg * dk
    sh_c = chunk.bit_length() - 1
    sh_d = dk.bit_length() - 1

    def iota(shape, dim):
        return lax.broadcasted_iota(jnp.int32, shape, dim)

    ri = iota((chunk, pk), 0)
    ci = iota((chunk, pk), 1) & (chunk - 1)
    incl = ri >= ci
    strict = ri > ci
    eye = jnp.where(ri == ci, 1.0, 0.0)[:, :pw]
    strict_g = strict[:, :pw]
    l_incl = jnp.where(iota((chunk, chunk), 0) >= iota((chunk, chunk), 1), 1.0, 0.0).astype(BF16)
    bm_sq = (iota((pw, pw), 0) >> sh_c) == (iota((pw, pw), 1) >> sh_c)
    bm_nt = (iota((pw, gw), 0) >> sh_c) == (iota((pw, gw), 1) >> sh_d)
    bm_nt2 = (iota((pw, 2 * gw), 0) >> sh_c) == ((iota((pw, 2 * gw), 1) >> sh_d) & (hpg - 1))
    n_sq = sh_c - 1
    gn = gn_ref[...]
    zero_bf = jnp.zeros((), BF16)

    def blockdiag(x_bf, mask):
        return jnp.where(mask, jnp.concatenate([x_bf] * hpg, axis=0), zero_bf)

    def dot3(ah, al, bh, bl):
        out = jnp.dot(ah, bh, preferred_element_type=F32)
        out = out + jnp.dot(ah, bl, preferred_element_type=F32)
        return out + jnp.dot(al, bh, preferred_element_type=F32)

    groups = range(ngrp)

    def chunk_body(c, carry):
        r0 = pl.multiple_of(c * chunk, chunk)
        rows = pl.ds(r0, chunk)
        qc = qs[rows, :]
        kc = ks[rows, :]
        vc = vs[rows, :]
        g = gx[rows, :]
        bt = bx[rows, :]
        zc = z_ref[0, rows, :]
        gcum = _dot_exact_lhs(l_incl, g)
        gdiff = _dot_exact_lhs(l_incl, jnp.where(strict, gpk[rows, :], 0.0))
        decay = jnp.where(incl, jnp.exp(gdiff), 0.0)
        bpack = bpk[rows, :]
        e_g = jnp.exp(gcum)
        g_last = gcum[chunk - 1:chunk, :]
        e_last = jnp.exp(g_last)
        kdec = kc * jnp.exp(g_last - gcum)
        rhs_v = bt * vc
        rhs_k = bt * e_g * kc
        qe = qc * e_g

        kq = []
        for gi in groups:
            cols = slice(gi * gw, (gi + 1) * gw)
            k_bf = kc[:, cols].astype(BF16)
            lhs = jnp.concatenate([kc[:, cols], qc[:, cols]], axis=0).astype(BF16)
            kq.append(lax.dot_general(lhs, blockdiag(k_bf, bm_nt), (((1,), (1,)), ((), ())),
                                      preferred_element_type=F32))
        dec_g = [decay[:, gi * pw:(gi + 1) * pw] for gi in groups]
        a_g = [jnp.where(strict_g, bpack[:, gi * pw:(gi + 1) * pw] * kq[gi][:chunk] * dec_g[gi], 0.0)
               for gi in groups]
        tinv = [eye - a for a in a_g]
        xs = [_split2(a) for a in a_g]
        xbd = [(blockdiag(xh, bm_sq), blockdiag(xl, bm_sq)) for xh, xl in xs]
        for _ in range(n_sq):
            x = [dot3(xs[gi][0], xs[gi][1], xbd[gi][0], xbd[gi][1]) for gi in groups]
            xs = [_split2(v) for v in x]
            xbd = [(blockdiag(xh, bm_sq), blockdiag(xl, bm_sq)) for xh, xl in xs]
            ts = [_split2(v) for v in tinv]
            tinv = [tinv[gi] + dot3(ts[gi][0], ts[gi][1], xbd[gi][0], xbd[gi][1]) for gi in groups]
        uw = []
        for gi in groups:
            cols = slice(gi * gw, (gi + 1) * gw)
            rh, rl = _split2(jnp.concatenate([rhs_v[:, cols], rhs_k[:, cols]], axis=1))
            th, tl = _split2(tinv[gi])
            uw.append(dot3(th, tl, blockdiag(rh, bm_nt2), blockdiag(rl, bm_nt2)))
        u = jnp.concatenate([uw[gi][:, :gw] for gi in groups], axis=1)
        w = jnp.concatenate([uw[gi][:, gw:] for gi in groups], axis=1)
        s_all = [s_scr[h] for h in range(heads)]
        wq = [_bdot(jnp.concatenate([w[:, h * dk:(h + 1) * dk], qe[:, h * dk:(h + 1) * dk]], axis=0), s_all[h])
              for h in range(heads)]
        v_new = jnp.concatenate([u[:, h * dk:(h + 1) * dk] - wq[h][:chunk] for h in range(heads)], axis=1)
        vn_bf = v_new.astype(BF16)
        o_intra = [jnp.dot((kq[gi][chunk:] * dec_g[gi]).astype(BF16),
                           blockdiag(vn_bf[:, gi * gw:(gi + 1) * gw], bm_nt), preferred_element_type=F32)
                   for gi in groups]
        upd = [_bdot_tn(kdec[:, h * dk:(h + 1) * dk], vn_bf[:, h * dk:(h + 1) * dk]) for h in range(heads)]
        for h in range(heads):
            sl = slice(h * dk, (h + 1) * dk)
            s_scr[h] = e_last[:, sl] * s_all[h] + upd[h]
            gi, hl = divmod(h, hpg)
            o_h = wq[h][chunk:] + o_intra[gi][:, hl * dk:(hl + 1) * dk]
            o_n = o_h * lax.rsqrt(jnp.mean(o_h * o_h, axis=-1, keepdims=True) + EPS) * gn
            o_ref[rows, sl] = o_n * _silu(zc[:, sl])
        return carry

    lax.fori_loop(0, tb // chunk, chunk_body, 0)
    s_out_ref[0] = s_scr[...]


def _gdn(p_all, ab, conv_w3, prm, gn, tail0, s0, *, n_seq, seq, tb, chunk, row_block_off, heads):
    hd = p_all.shape[2]
    dk = hd // heads
    nblk = seq // tb
    conv_w = conv_w3.shape[0]
    assert seq % tb == 0 and tb % chunk == 0 and tb >= SUBLANES and chunk & (chunk - 1) == 0
    assert conv_w - 1 <= SUBLANES and heads * chunk % LANES == 0

    def plane(p):
        return pl.BlockSpec((1, tb, hd), lambda b, t: (p, row_block_off + b * nblk + t, 0))

    kern = functools.partial(_gdn_kernel, chunk=chunk, heads=heads, conv_w=conv_w)
    return pl.pallas_call(
        kern,
        out_shape=(
            jax.ShapeDtypeStruct((n_seq * seq, hd), F32),
            jax.ShapeDtypeStruct((n_seq, heads, dk, dk), F32),
            jax.ShapeDtypeStruct((n_seq, 3, SUBLANES, hd), F32),
        ),
        grid=(n_seq, nblk),
        in_specs=[
            plane(0), plane(1), plane(2), plane(3),
            pl.BlockSpec((tb, LANES), lambda b, t: (row_block_off + b * nblk + t, 0)),
            pl.BlockSpec((conv_w, 3, hd), lambda b, t: (0, 0, 0)),
            pl.BlockSpec((SUBLANES, LANES), lambda b, t: (0, 0)),
            pl.BlockSpec((1, dk), lambda b, t: (0, 0)),
            pl.BlockSpec((3, SUBLANES, hd), lambda b, t: (0, 0, 0)),
            pl.BlockSpec((heads, dk, dk), lambda b, t: (0, 0, 0)),
        ],
        out_specs=(
            pl.BlockSpec((tb, hd), lambda b, t: (b * nblk + t, 0)),
            pl.BlockSpec((1, heads, dk, dk), lambda b, t: (b, 0, 0, 0)),
            pl.BlockSpec((1, 3, SUBLANES, hd), lambda b, t: (b, 0, 0, 0)),
        ),
        scratch_shapes=[
            pltpu.VMEM((3, tb + SUBLANES, hd), F32),
            pltpu.VMEM((tb, hd), F32), pltpu.VMEM((tb, hd), F32), pltpu.VMEM((tb, hd), F32),
            pltpu.VMEM((tb, hd), F32), pltpu.VMEM((tb, hd), F32),
            pltpu.VMEM((tb, heads * chunk), F32), pltpu.VMEM((tb, heads * chunk), F32),
            pltpu.VMEM((heads, dk, dk), F32),
        ],
        compiler_params=_cparams(("parallel", "arbitrary")),
        name=f"gdn_chunk{chunk}",
    )(p_all, p_all, p_all, p_all, ab, conv_w3, prm, gn, tail0, s0)


def _sb_prompt_kernel(bias_ref, q_ref, k_ref, v_ref, km_ref, vm_ref, o_ref, kbf, vtb, *, scale, tq):
    seq = k_ref.shape[1]
    dh = k_ref.shape[2]
    n_meta = km_ref.shape[0]
    nkb = seq // tq
    h = pl.program_id(1)
    qi = pl.program_id(2)

    @pl.when(qi == 0)
    def _():
        for j in range(nkb):
            kbf[j] = k_ref[0, j * tq:(j + 1) * tq, :].astype(BF16)
            vtb[j] = v_ref[0, j * tq:(j + 1) * tq, :].T.astype(BF16)

    bias = bias_ref[h]
    qt = (q_ref[0] * scale).T.astype(BF16)

    def suffix_mat(n):
        r = lax.broadcasted_iota(jnp.int32, (n, n), 0)
        c = lax.broadcasted_iota(jnp.int32, (n, n), 1)
        return jnp.where(c > r, 1.0, 0.0).astype(BF16)

    u_full = suffix_mat(tq)

    def logits(tiles):
        zt = [jnp.dot(k_bf, qt, preferred_element_type=F32) + bias for k_bf, _, _ in tiles]
        sp = [_softplus(z) for z in zt]
        ls = [s if t[2] is None else jnp.where(t[2], s, 0.0) for s, t in zip(sp, tiles)]
        parts = [_split2(x) for x in ls]
        after = [jnp.dot(t[1], p[0], preferred_element_type=F32) for t, p in zip(tiles, parts)]
        after = [a + jnp.dot(t[1], p[1], preferred_element_type=F32) for a, t, p in zip(after, tiles, parts)]
        return [(z - s, l, a, t[2]) for z, s, l, a, t in zip(zt, sp, ls, after, tiles)]

    def attend(pre, vts, carry, acc):
        a_bf = []
        for (lz, ls, after, mask) in pre:
            a = jnp.exp(lz - after - carry)
            if mask is not None:
                a = jnp.where(mask, a, 0.0)
            a_bf.append(a.astype(BF16))
            carry = carry + after[0:1, :] + ls[0:1, :]
        for a, vt_bf in zip(a_bf, vts):
            acc = acc + jnp.dot(vt_bf, a, preferred_element_type=F32)
        return carry, acc

    r = lax.broadcasted_iota(jnp.int32, (tq, tq), 0)
    c = lax.broadcasted_iota(jnp.int32, (tq, tq), 1)
    pre_diag, pre_meta = logits([(kbf[qi], u_full, r < c), (km_ref[...].astype(BF16), suffix_mat(n_meta), None)])
    carry, acc = attend([pre_diag], [vtb[qi]], jnp.zeros((1, tq), F32), jnp.zeros((dh, tq), F32))

    def pair(i, ca):
        kb = qi - 1 - 2 * i
        pre = logits([(kbf[kb], u_full, None), (kbf[kb - 1], u_full, None)])
        return attend(pre, [vtb[kb], vtb[kb - 1]], ca[0], ca[1])

    carry, acc = lax.fori_loop(0, qi // 2, pair, (carry, acc))

    def last_single(ca):
        return attend(logits([(kbf[0], u_full, None)]), [vtb[0]], ca[0], ca[1])

    carry, acc = lax.cond(qi % 2 == 1, last_single, lambda ca: ca, (carry, acc))
    carry, acc = attend([pre_meta], [vm_ref[...].T.astype(BF16)], carry, acc)
    o_ref[...] = acc.T


def _sb_prompt(p_all, k_meta, v_meta, bias, *, n_seq, seq, heads, q_plane, tq):
    hd = p_all.shape[2]
    dh = hd // heads
    nq = seq // tq
    n_meta = k_meta.shape[0]
    assert seq % tq == 0
    kern = functools.partial(_sb_prompt_kernel, scale=dh ** -0.5, tq=tq)
    return pl.pallas_call(
        kern,
        out_shape=jax.ShapeDtypeStruct((n_seq * seq, hd), F32),
        grid=(n_seq, heads, nq),
        in_specs=[
            pl.BlockSpec(memory_space=pltpu.SMEM),
            pl.BlockSpec((1, tq, dh), lambda b, h, i: (q_plane, b * nq + i, h)),
            pl.BlockSpec((1, seq, dh), lambda b, h, i: (q_plane + 1, b, h)),
            pl.BlockSpec((1, seq, dh), lambda b, h, i: (q_plane + 2, b, h)),
            pl.BlockSpec((n_meta, dh), lambda b, h, i: (0, h)),
            pl.BlockSpec((n_meta, dh), lambda b, h, i: (0, h)),
        ],
        out_specs=pl.BlockSpec((tq, dh), lambda b, h, i: (b * nq + i, h)),
        scratch_shapes=[pltpu.VMEM((nq, tq, dh), BF16), pltpu.VMEM((nq, dh, tq), BF16)],
        compiler_params=_cparams(("parallel", "parallel", "arbitrary")),
        name="sb_prompt",
    )(bias, p_all, p_all, p_all, k_meta, v_meta)


def _merge_kernel(oa_ref, ob_ref, ga_ref, gb_ref, x_ref, wpa_ref, wpb_ref, wo_ref, out_ref):
    pa = jnp.dot(oa_ref[...].astype(BF16), wpa_ref[...], preferred_element_type=F32)
    pb = jnp.dot(ob_ref[...].astype(BF16), wpb_ref[...], preferred_element_type=F32)
    m = _sigmoid(ga_ref[0]) * pa + _sigmoid(gb_ref[0]) * pb
    out_ref[...] = x_ref[...] + jnp.dot(m.astype(BF16), wo_ref[...], preferred_element_type=F32)


def _merge(o_a, o_b, p_all, x, w_pa, w_pb, w_o, *, gate_plane, tm):
    m, d = x.shape
    da = o_a.shape[1]
    db = o_b.shape[1]
    assert m % tm == 0

    def whole(shape):
        return pl.BlockSpec(shape, lambda i: (0, 0))

    return pl.pallas_call(
        _merge_kernel,
        out_shape=jax.ShapeDtypeStruct((m, d), F32),
        grid=(m // tm,),
        in_specs=[
            pl.BlockSpec((tm, da), lambda i: (i, 0)),
            pl.BlockSpec((tm, db), lambda i: (i, 0)),
            pl.BlockSpec((1, tm, d), lambda i: (gate_plane, i, 0)),
            pl.BlockSpec((1, tm, d), lambda i: (gate_plane + 1, i, 0)),
            pl.BlockSpec((tm, d), lambda i: (i, 0)),
            whole(w_pa.shape), whole(w_pb.shape), whole(w_o.shape),
        ],
        out_specs=pl.BlockSpec((tm, d), lambda i: (i, 0)),
        compiler_params=_cparams(("parallel",)),
        name="merge",
    )(o_a, o_b, p_all, p_all, x, w_pa, w_pb, w_o)


def _ffn_kernel(h_ref, nw_ref, wg_ref, wu_ref, wo_ref, nf_ref, y_ref, xn_ref, acc_ref):
    f = pl.program_id(1)

    @pl.when(f == 0)
    def _():
        h = h_ref[...]
        xn_ref[...] = _rms(h, nw_ref[...]).astype(BF16)
        acc_ref[...] = h

    xn = xn_ref[...]
    gt = jnp.dot(xn, wg_ref[...], preferred_element_type=F32)
    up = jnp.dot(xn, wu_ref[...], preferred_element_type=F32)
    acc_ref[...] += jnp.dot((_silu(gt) * up).astype(BF16), wo_ref[...], preferred_element_type=F32)

    @pl.when(f == pl.num_programs(1) - 1)
    def _():
        y_ref[...] = _rms(acc_ref[...], nf_ref[...])


def _ffn(h, norm_w, w_in, w_out, norm_f, *, tm, tf):
    m, d = h.shape
    dff = w_out.shape[0]
    nf = dff // tf
    assert m % tm == 0 and dff % tf == 0
    return pl.pallas_call(
        _ffn_kernel,
        out_shape=jax.ShapeDtypeStruct((m, d), F32),
        grid=(m // tm, nf),
        in_specs=[
            pl.BlockSpec((tm, d), lambda i, f: (i, 0)),
            pl.BlockSpec((1, d), lambda i, f: (0, 0)),
            pl.BlockSpec((d, tf), lambda i, f: (0, f)),
            pl.BlockSpec((d, tf), lambda i, f: (0, f + nf)),
            pl.BlockSpec((tf, d), lambda i, f: (f, 0)),
            pl.BlockSpec((1, d), lambda i, f: (0, 0)),
        ],
        out_specs=pl.BlockSpec((tm, d), lambda i, f: (i, 0)),
        scratch_shapes=[pltpu.VMEM((tm, d), BF16), pltpu.VMEM((tm, d), F32)],
        compiler_params=_cparams(("parallel", "arbitrary")),
        name="ffn",
    )(h, norm_w, w_in, w_in, w_out, norm_f)


def _gdn_step_kernel(q_ref, k_ref, v_ref, z_ref, ab_ref, cw_ref, prm_ref, gn_ref, buf_ref, s_ref,
                     o_ref, s_out_ref, qs, ks, vs, gx, bx, *, heads, conv_w):
    nb = qs.shape[0]
    hd = qs.shape[1]
    dk = hd // heads
    b = pl.program_id(0)

    @pl.when(b == 0)
    def _():
        srcs = (q_ref, k_ref, v_ref)
        dsts = (qs, ks, vs)
        for p in range(3):
            y = cw_ref[conv_w - 1, p:p + 1, :] * srcs[p][0, 0:nb, :]
            for i in range(conv_w - 1):
                y = y + cw_ref[i, p:p + 1, :] * buf_ref[i, p]
            c = _silu(y)
            if p == 2:
                dsts[p][...] = c
            else:
                scale = dk ** -0.5 if p == 0 else 1.0
                for h in range(heads):
                    ch = c[:, h * dk:(h + 1) * dk]
                    inv = lax.rsqrt(jnp.sum(ch * ch, axis=-1, keepdims=True) + EPS)
                    dsts[p][:, h * dk:(h + 1) * dk] = ch * (inv * scale)
        ab = ab_ref[0:nb, :]
        g_t = -jnp.exp(prm_ref[0:1, :]) * _softplus(ab + prm_ref[1:2, :])
        b_t = _sigmoid(ab)
        gx[...] = _dot_exact_rhs(g_t, _head_select(hd, dk))
        bx[...] = _dot_exact_rhs(b_t, _head_select(hd, dk, heads))

    rb = pl.ds(b, 1)
    qrow = qs[rb, :]
    krow = ks[rb, :]
    vrow = vs[rb, :]
    e_g = jnp.exp(gx[rb, :])
    beta = bx[rb, :]
    zrow = z_ref[0, rb, :]
    gn = gn_ref[...]
    first = lax.broadcasted_iota(jnp.int32, (SUBLANES, dk), 0) == 0
    for h in range(heads):
        sl = slice(h * dk, (h + 1) * dk)
        s = s_ref[0, h]
        kh = krow[:, sl]
        qh = qrow[:, sl]
        lhs = jnp.where(first, kh, qh)
        prod = _dot_hi(jnp.broadcast_to(lhs, (SUBLANES, dk)), s)
        k_s = prod[0:1, :]
        q_s = prod[1:2, :]
        v_new = beta[:, sl] * (vrow[:, sl] - e_g[:, sl] * k_s)
        qk = jnp.sum(qh * kh, axis=-1, keepdims=True)
        o_h = e_g[:, sl] * q_s + qk * v_new
        k8 = jnp.where(first, kh, 0.0)
        v8 = jnp.where(first, v_new, 0.0)
        kh_hi, kh_lo = _split2(k8)
        vn_hi, vn_lo = _split2(v8)
        outer = (_bdot_tn(kh_hi, vn_hi) + _bdot_tn(kh_hi, vn_lo)) + _bdot_tn(kh_lo, vn_hi)
        s_out_ref[0, h] = e_g[:, sl] * s + outer
        o_n = o_h * lax.rsqrt(jnp.mean(o_h * o_h, axis=-1, keepdims=True) + EPS) * gn
        o_ref[0, :, sl] = o_n * _silu(zrow[:, sl])


def _gdn_step(p_small, ab, conv_w3, prm, gn, buf, state, *, heads):
    nb = state.shape[0]
    ms, hd = p_small.shape[1], p_small.shape[2]
    dk = hd // heads
    conv_w = conv_w3.shape[0]
    kern = functools.partial(_gdn_step_kernel, heads=heads, conv_w=conv_w)

    def plane(p):
        return pl.BlockSpec((1, ms, hd), lambda b: (p, 0, 0))

    return pl.pallas_call(
        kern,
        out_shape=(jax.ShapeDtypeStruct((nb, 1, hd), F32), jax.ShapeDtypeStruct(state.shape, F32)),
        grid=(nb,),
        in_specs=[
            plane(0), plane(1), plane(2), plane(3),
            pl.BlockSpec((ms, LANES), lambda b: (0, 0)),
            pl.BlockSpec((conv_w, 3, hd), lambda b: (0, 0, 0)),
            pl.BlockSpec((SUBLANES, LANES), lambda b: (0, 0)),
            pl.BlockSpec((1, dk), lambda b: (0, 0)),
            pl.BlockSpec((conv_w - 1, 3, nb, hd), lambda b: (0, 0, 0, 0)),
            pl.BlockSpec((1, heads, dk, dk), lambda b: (b, 0, 0, 0)),
        ],
        out_specs=(
            pl.BlockSpec((1, 1, hd), lambda b: (b, 0, 0)),
            pl.BlockSpec((1, heads, dk, dk), lambda b: (b, 0, 0, 0)),
        ),
        scratch_shapes=[pltpu.VMEM((nb, hd), F32) for _ in range(5)],
        compiler_params=_cparams(("arbitrary",)),
        name="gdn_step",
    )(p_small, p_small, p_small, p_small, ab, conv_w3, prm, gn, buf, state)


def _sb_step_kernel(pt_ref, q_ref, kn_ref, vn_ref, bias_ref, *refs, scale, pages_per_step, heads, past_len):
    k_refs = refs[:pages_per_step]
    v_refs = refs[pages_per_step:2 * pages_per_step]
    o_ref = refs[2 * pages_per_step]
    qbd, acc, carry = refs[2 * pages_per_step + 1:]
    dh = k_refs[0].shape[2]
    page = k_refs[0].shape[1] // heads
    hd = heads * dh
    b = pl.program_id(0)
    j = pl.program_id(1)
    lane_head = lax.broadcasted_iota(jnp.int32, (dh, LANES), 1)

    @pl.when(j == 0)
    def _():
        qrow = q_ref[pl.ds(b, 1), :] * scale
        for h in range(heads):
            qh_t = jnp.broadcast_to(qrow[:, h * dh:(h + 1) * dh], (LANES, dh)).T
            qbd[h * dh:(h + 1) * dh, :] = jnp.where(lane_head == h, qh_t, 0.0).astype(BF16)
        acc[...] = jnp.zeros_like(acc)
        carry[...] = jnp.zeros_like(carry)

    def head_major(ref):
        return jnp.concatenate([ref[0, pl.ds(h, page, stride=heads), :] for h in range(heads)], axis=1)

    r = lax.broadcasted_iota(jnp.int32, (page, page), 0)
    c = lax.broadcasted_iota(jnp.int32, (page, page), 1)
    u_bf = jnp.where(c > r, 1.0, 0.0).astype(BF16)
    expand = _head_select(hd, dh)
    bias = bias_ref[...]
    n = pages_per_step
    z = [jnp.dot(head_major(k_refs[i]).astype(BF16), qbd[...], preferred_element_type=F32) + bias
         for i in range(n)]
    sp = [_softplus(z[i]) for i in range(n)]
    parts = [_split2(sp[i]) for i in range(n)]
    after = [jnp.dot(u_bf, parts[i][0], preferred_element_type=F32) for i in range(n)]
    after = [after[i] + jnp.dot(u_bf, parts[i][1], preferred_element_type=F32) for i in range(n)]
    cur = carry[...]
    a = []
    for i in range(n):
        a.append(jnp.exp(z[i] - sp[i] - after[i] - cur).astype(BF16))
        cur = cur + after[i][0:1, :] + sp[i][0:1, :]
    carry[...] = cur
    a_x = [jnp.dot(a[i], expand, preferred_element_type=F32) for i in range(n)]
    contrib = a_x[0] * head_major(v_refs[0])
    for i in range(1, n):
        contrib = contrib + a_x[i] * head_major(v_refs[i])
    acc[...] += contrib

    @pl.when(j == pl.num_programs(1) - 1)
    def _():
        out = jnp.sum(acc[...], axis=0, keepdims=True)
        qrow = q_ref[pl.ds(b, 1), :] * scale
        krow = kn_ref[pl.ds(b, 1), :]
        vrow = vn_ref[pl.ds(b, 1), :]
        self_valid = past_len < past_len
        for h in range(heads):
            sl = slice(h * dh, (h + 1) * dh)
            z_new = jnp.sum(qrow[:, sl] * krow[:, sl], axis=-1, keepdims=True) + bias[:, h:h + 1]
            a_new = jnp.where(self_valid, jnp.exp(z_new - _softplus(z_new)), 0.0)
            o_ref[0, :, sl] = out[:, sl] + a_new * vrow[:, sl]


def _sb_step(page_table, q, k_new, v_new, bias_row, cache_k, cache_v, *, heads, pages_per_step):
    nb, n_pages = page_table.shape
    rows, dh = cache_k.shape[1], cache_k.shape[2]
    page = rows // heads
    hd = heads * dh
    nsteps = n_pages // pages_per_step
    assert n_pages % pages_per_step == 0
    kern = functools.partial(_sb_step_kernel, scale=dh ** -0.5, pages_per_step=pages_per_step,
                             heads=heads, past_len=n_pages * page)

    def page_spec(i):
        return pl.BlockSpec((1, rows, dh), lambda b, j, pt: (pt[b, n_pages - 1 - (j * pages_per_step + i)], 0, 0))

    def whole(shape):
        return pl.BlockSpec(shape, lambda b, j, pt: (0, 0))

    grid_spec = pltpu.PrefetchScalarGridSpec(
        num_scalar_prefetch=1,
        grid=(nb, nsteps),
        in_specs=[whole(q.shape), whole(k_new.shape), whole(v_new.shape), whole(bias_row.shape)]
        + [page_spec(i) for i in range(pages_per_step)] * 2,
        out_specs=pl.BlockSpec((1, 1, hd), lambda b, j, pt: (b, 0, 0)),
        scratch_shapes=[pltpu.VMEM((hd, LANES), BF16), pltpu.VMEM((page, hd), F32), pltpu.VMEM((1, LANES), F32)],
    )
    out = pl.pallas_call(
        kern,
        out_shape=jax.ShapeDtypeStruct((nb, 1, hd), F32),
        grid_spec=grid_spec,
        compiler_params=_cparams(("parallel", "arbitrary")),
        name="sb_step",
    )(page_table, q, k_new, v_new, bias_row, *([cache_k] * pages_per_step), *([cache_v] * pages_per_step))
    return out.reshape(nb, hd)


def _layer(l, x_prompt, x_sample, cache_k, cache_v, state_rec, state_conv, page_table, meta_tokens,
           norm_mix, w_in, conv_w, a_log, dt_bias, gdn_norm, sb_bias, w_pa, w_pb, w_o,
           norm_ffn, w_ffn_in, w_ffn_out, norm_final):
    n_seq, seq, d = x_prompt.shape
    nb = x_sample.shape[0]
    n_meta = meta_tokens.shape[0]
    heads_a = a_log.shape[1]
    heads_b = sb_bias.shape[1]
    d_qkv = conv_w.shape[2]
    d_a = d_qkv // 3
    d_b = w_pb.shape[1]
    conv_width = conv_w.shape[1]
    dk = d_a // heads_a
    assert d_a == d and d_b == d and 2 * heads_a <= LANES and nb % n_meta == 0

    w = w_in[l]
    c_ab = d_qkv + d_a
    w_main = jnp.concatenate([w[:, :c_ab], w[:, c_ab + 2 * heads_a:]], axis=1).astype(BF16)
    w_ab = jnp.pad(w[:, c_ab:c_ab + 2 * heads_a], ((0, 0), (0, LANES - 2 * heads_a))).astype(BF16)
    nm = norm_mix[l].reshape(1, d)
    conv_w3 = conv_w[l].reshape(conv_width, 3, d_a)
    prm = jnp.zeros((SUBLANES, LANES), F32).at[0, :heads_a].set(a_log[l]).at[1, :heads_a].set(dt_bias[l])
    gn = gdn_norm[l].reshape(1, dk)
    bias = sb_bias[l]
    wpa, wpb, wo = w_pa[l].astype(BF16), w_pb[l].astype(BF16), w_o[l].astype(BF16)
    wfi, wfo = w_ffn_in[l].astype(BF16), w_ffn_out[l].astype(BF16)
    nffn = norm_ffn[l].reshape(1, d)
    nfin = norm_final.reshape(1, d)

    ms = -(-(nb + n_meta) // 64) * 64
    x_small = jnp.concatenate([x_sample.reshape(nb, d), meta_tokens.astype(F32),
                               jnp.zeros((ms - nb - n_meta, d), F32)], axis=0)
    p_small, ab_small = _inproj(x_small, nm, w_main, w_ab, tm=ms)

    zeros_tail = jnp.zeros((3, SUBLANES, d_a), F32)
    zeros_state = jnp.zeros((heads_a, dk, dk), F32)
    _, s_meta, tail_meta = _gdn(p_small, ab_small, conv_w3, prm, gn, zeros_tail, zeros_state,
                                n_seq=1, seq=n_meta, tb=n_meta, chunk=n_meta,
                                row_block_off=nb // n_meta, heads=heads_a)
    k_meta = p_small[5, nb:nb + n_meta]
    v_meta = p_small[6, nb:nb + n_meta]

    xp = x_prompt.reshape(n_seq * seq, d)
    p_all, ab_all = _inproj(xp, nm, w_main, w_ab, tm=1024)
    o_a, s_prompt, tail_prompt = _gdn(p_all, ab_all, conv_w3, prm, gn, tail_meta[0], s_meta[0],
                                      n_seq=n_seq, seq=seq, tb=256, chunk=GDN_CHUNK,
                                      row_block_off=0, heads=heads_a)
    o_b = _sb_prompt(p_all, k_meta, v_meta, bias, n_seq=n_seq, seq=seq, heads=heads_b, q_plane=4, tq=256)
    h1 = _merge(o_a, o_b, p_all, xp, wpa, wpb, wo, gate_plane=7, tm=512)
    y_prompt = _ffn(h1, nffn, wfi, wfo, nfin, tm=512, tf=1408).reshape(n_seq, seq, d)

    dh = d_b // heads_b
    k_prompt = jnp.concatenate([jnp.broadcast_to(k_meta[None], (n_seq, n_meta, d_b)),
                                p_all[5].reshape(n_seq, seq, d_b)], axis=1).reshape(n_seq, n_meta + seq, heads_b, dh)
    v_prompt = jnp.concatenate([jnp.broadcast_to(v_meta[None], (n_seq, n_meta, d_b)),
                                p_all[6].reshape(n_seq, seq, d_b)], axis=1).reshape(n_seq, n_meta + seq, heads_b, dh)
    conv_prompt = jnp.transpose(tail_prompt[:, :, SUBLANES - (conv_width - 1):, :], (0, 2, 1, 3)).reshape(
        n_seq, conv_width - 1, d_qkv)

    buf = jnp.transpose(state_conv[l].reshape(nb, conv_width - 1, 3, d_a), (1, 2, 0, 3))
    o_a_s, s_sample = _gdn_step(p_small, ab_small, conv_w3, prm, gn, buf, state_rec[l], heads=heads_a)
    o_a_s = o_a_s.reshape(nb, d_a)
    n_phys, page = cache_k.shape[1], cache_k.shape[2]
    bias_row = jnp.zeros((1, LANES), F32).at[0, :heads_b].set(bias)
    o_b_s = _sb_step(page_table, p_small[4, :nb], p_small[5, :nb], p_small[6, :nb], bias_row,
                     cache_k[l].reshape(n_phys, page * heads_b, dh), cache_v[l].reshape(n_phys, page * heads_b, dh),
                     heads=heads_b, pages_per_step=4)
    h1_s = _merge(o_a_s, o_b_s, p_small, x_small[:nb], wpa, wpb, wo, gate_plane=7, tm=nb)
    y_sample = _ffn(h1_s, nffn, wfi, wfo, nfin, tm=nb, tf=1408).reshape(nb, 1, d)

    k_sample = p_small[5, :nb].reshape(nb, 1, heads_b, dh)
    v_sample = p_small[6, :nb].reshape(nb, 1, heads_b, dh)
    qkv_new = jnp.transpose(p_small[0:3, :nb], (1, 0, 2)).reshape(nb, 1, d_qkv)
    conv_sample = jnp.concatenate([state_conv[l][:, 1:], qkv_new], axis=1)
    return (y_prompt, y_sample, k_prompt, v_prompt, k_sample, v_sample,
            s_prompt, s_sample, conv_prompt, conv_sample)


def kernel(x_prompt, x_sample, cache_k, cache_v, state_rec, state_conv, page_table, meta_tokens, norm_mix, w_in, conv_w, a_log, dt_bias, gdn_norm, sb_bias, w_pa, w_pb, w_o, norm_ffn, w_ffn_in, w_ffn_out, norm_final):
    depth = w_in.shape[0]
    assert depth == 1 and x_sample.shape[1] == 1, "single layer, single decode token per sequence"
    outs = _layer(0, x_prompt, x_sample, cache_k, cache_v, state_rec, state_conv, page_table, meta_tokens,
                  norm_mix, w_in, conv_w, a_log, dt_bias, gdn_norm, sb_bias, w_pa, w_pb, w_o,
                  norm_ffn, w_ffn_in, w_ffn_out, norm_final)
    (y_prompt, y_sample, k_p, v_p, k_s, v_s, rec_p, rec_s, conv_p, conv_s) = outs
    return (y_prompt, y_sample, k_p[None], v_p[None], k_s[None], v_s[None],
            rec_p[None], rec_s[None], conv_p[None], conv_s[None])
```

```python
import functools

import jax
import jax.numpy as jnp
from jax import lax
from jax.experimental import pallas as pl
from jax.experimental.pallas import tpu as pltpu

F32 = jnp.float32
BF16 = jnp.bfloat16
EPS = 1e-6
LANES = 128
SUBLANES = 8
MXU_DIM = 256
GDN_CHUNK = 64
VMEM_LIMIT = 56 * 1024 * 1024


def _cparams(sem):
    return pltpu.CompilerParams(dimension_semantics=sem, vmem_limit_bytes=VMEM_LIMIT)


def _bdot(a, b):
    return jnp.dot(a.astype(BF16), b.astype(BF16), preferred_element_type=F32)


def _bdot_nt(a, b):
    return lax.dot_general(a.astype(BF16), b.astype(BF16), (((1,), (1,)), ((), ())),
                           preferred_element_type=F32)


def _bdot_tn(a, b):
    return lax.dot_general(a.astype(BF16), b.astype(BF16), (((0,), (0,)), ((), ())),
                           preferred_element_type=F32)


def _split2(x):
    hi = x.astype(BF16)
    lo = (x - hi.astype(F32)).astype(BF16)
    return hi, lo


def _split3(x):
    hi = x.astype(BF16)
    r = x - hi.astype(F32)
    mid = r.astype(BF16)
    lo = (r - mid.astype(F32)).astype(BF16)
    return hi, mid, lo


def _dot_exact_lhs(lhs_bf, x, passes=3):
    parts = _split3(x) if passes == 3 else _split2(x)
    out = jnp.dot(lhs_bf, parts[0], preferred_element_type=F32)
    for p in parts[1:]:
        out = out + jnp.dot(lhs_bf, p, preferred_element_type=F32)
    return out


def _dot_exact_rhs(x, rhs_bf, passes=3):
    parts = _split3(x) if passes == 3 else _split2(x)
    out = jnp.dot(parts[0], rhs_bf, preferred_element_type=F32)
    for p in parts[1:]:
        out = out + jnp.dot(p, rhs_bf, preferred_element_type=F32)
    return out


def _dot_hi(a, b):
    ah, al = _split2(a)
    bh, bl = _split2(b)
    out = jnp.dot(ah, bh, preferred_element_type=F32)
    out = out + jnp.dot(ah, bl, preferred_element_type=F32)
    return out + jnp.dot(al, bh, preferred_element_type=F32)


def _softplus(x):
    return jnp.maximum(x, 0.0) + jnp.log(1.0 + jnp.exp(-jnp.abs(x)))


def _sigmoid(x):
    return 1.0 / (1.0 + jnp.exp(-x))


def _silu(x):
    return x * _sigmoid(x)


def _rms(x, w):
    return x * lax.rsqrt(jnp.mean(x * x, axis=-1, keepdims=True) + EPS) * w


def _head_select(width, head_dim, lane_offset=0):
    shift = head_dim.bit_length() - 1
    assert 1 << shift == head_dim
    row = lax.broadcasted_iota(jnp.int32, (LANES, width), 0)
    col_head = lax.shift_right_logical(lax.broadcasted_iota(jnp.int32, (LANES, width), 1), shift)
    return jnp.where(row == col_head + lane_offset, 1.0, 0.0).astype(BF16)


def _inproj_kernel(x_ref, nw_ref, w_ref, wab_ref, out_ref, ab_ref, xn_ref):
    @pl.when(pl.program_id(1) == 0)
    def _():
        xn_ref[...] = _rms(x_ref[...], nw_ref[...]).astype(BF16)
        ab_ref[...] = jnp.dot(xn_ref[...], wab_ref[...], preferred_element_type=F32)

    out_ref[0] = jnp.dot(xn_ref[...], w_ref[...], preferred_element_type=F32)


def _inproj(x, norm_w, w_main, w_ab, tm):
    m, d = x.shape
    n = w_main.shape[1]
    tn = d
    nj = n // tn
    assert m % tm == 0 and n % tn == 0
    return pl.pallas_call(
        _inproj_kernel,
        out_shape=(jax.ShapeDtypeStruct((nj, m, tn), F32), jax.ShapeDtypeStruct((m, LANES), F32)),
        grid=(m // tm, nj),
        in_specs=[
            pl.BlockSpec((tm, d), lambda i, j: (i, 0)),
            pl.BlockSpec((1, d), lambda i, j: (0, 0)),
            pl.BlockSpec((d, tn), lambda i, j: (0, j)),
            pl.BlockSpec((d, LANES), lambda i, j: (0, 0)),
        ],
        out_specs=(
            pl.BlockSpec((1, tm, tn), lambda i, j: (j, i, 0)),
            pl.BlockSpec((tm, LANES), lambda i, j: (i, 0)),
        ),
        scratch_shapes=[pltpu.VMEM((tm, d), BF16)],
        compiler_params=_cparams(("parallel", "arbitrary")),
        name="inproj",
    )(x, norm_w, w_main, w_ab)


def _gdn_kernel(q_ref, k_ref, v_ref, z_ref, ab_ref, cw_ref, prm_ref, gn_ref, tail0_ref, s0_ref,
                o_ref, s_out_ref, tail_out_ref,
                xpad, qs, ks, vs, gx, bx, gpk, bpk, s_scr, m_sq, m_nt, *, chunk, heads, conv_w):
    ns = q_ref.shape[1]
    tb = q_ref.shape[2]
    hd = q_ref.shape[3]
    dk = hd // heads
    pk = heads * chunk
    t = pl.program_id(1)

    @pl.when(t == 0)
    def _():
        for s in range(ns):
            s_scr[s] = s0_ref[...]
            xpad[s, :, 0:SUBLANES, :] = tail0_ref[...]

    srcs = (q_ref, k_ref, v_ref)
    dsts = (qs, ks, vs)
    sel_g, sel_b = _head_select(hd, dk), _head_select(hd, dk, heads)
    sel_gp, sel_bp = _head_select(pk, chunk), _head_select(pk, chunk, heads)
    for s in range(ns):
        for p in range(3):
            xpad[s, p, SUBLANES:SUBLANES + tb, :] = srcs[p][0, s]
        for p in range(3):
            y = None
            for i in range(conv_w):
                term = cw_ref[i, p:p + 1, :] * xpad[s, p, pl.ds(SUBLANES - (conv_w - 1) + i, tb), :]
                y = term if y is None else y + term
            c = _silu(y)
            if p == 2:
                dsts[p][s] = c
            else:
                scale = dk ** -0.5 if p == 0 else 1.0
                for h in range(heads):
                    ch = c[:, h * dk:(h + 1) * dk]
                    inv = lax.rsqrt(jnp.sum(ch * ch, axis=-1, keepdims=True) + EPS)
                    dsts[p][s, :, h * dk:(h + 1) * dk] = ch * (inv * scale)
        new_tail = xpad[s, :, tb:tb + SUBLANES, :]
        xpad[s, :, 0:SUBLANES, :] = new_tail
        tail_out_ref[s] = new_tail

        ab = ab_ref[s]
        g_t = -jnp.exp(prm_ref[0:1, :]) * _softplus(ab + prm_ref[1:2, :])
        b_t = _sigmoid(ab)
        gx[s] = _dot_exact_rhs(g_t, sel_g)
        bx[s] = _dot_exact_rhs(b_t, sel_b)
        gpk[s] = _dot_exact_rhs(g_t, sel_gp)
        bpk[s] = _dot_exact_rhs(b_t, sel_bp)

    hpg = min(heads, MXU_DIM // chunk)
    ngrp = heads // hpg
    pw = hpg * chunk
    gw = hpg * dk
    sh_c = chunk.bit_length() - 1
    sh_d = dk.bit_length() - 1

    def iota(shape, dim):
        return lax.broadcasted_iota(jnp.int32, shape, dim)

    ri = iota((chunk, pk), 0)
    ci = iota((chunk, pk), 1) & (chunk - 1)
    incl = ri >= ci
    strict = ri > ci
    eye = jnp.where(ri == ci, 1.0, 0.0)[:, :pw]
    strict_g = strict[:, :pw]
    l_incl = jnp.where(iota((chunk, chunk), 0) >= iota((chunk, chunk), 1), 1.0, 0.0).astype(BF16)
    @pl.when(t == 0)
    def _():
        m_sq[...] = jnp.where((iota((pw, pw), 0) >> sh_c) == (iota((pw, pw), 1) >> sh_c), 1.0, 0.0).astype(BF16)
        m_nt[...] = jnp.where((iota((pw, 2 * gw), 0) >> sh_c) == ((iota((pw, 2 * gw), 1) >> sh_d) & (hpg - 1)),
                              1.0, 0.0).astype(BF16)

    n_sq = sh_c - 1
    gn = gn_ref[...]

    def blockdiag(x_bf, mask_ref):
        return jnp.concatenate([x_bf] * hpg, axis=0) * mask_ref[:, :x_bf.shape[1]]

    def dot3(ah, al, bh, bl):
        out = jnp.dot(ah, bh, preferred_element_type=F32)
        out = out + jnp.dot(ah, bl, preferred_element_type=F32)
        return out + jnp.dot(al, bh, preferred_element_type=F32)

    seqs = range(ns)
    units = [(s, gi) for s in seqs for gi in range(ngrp)]
    sheads = [(s, h) for s in seqs for h in range(heads)]

    def gcols(gi):
        return slice(gi * gw, (gi + 1) * gw)

    def hcols(h):
        return slice(h * dk, (h + 1) * dk)

    def chunk_body(c, carry):
        r0 = pl.multiple_of(c * chunk, chunk)
        rows = pl.ds(r0, chunk)
        qc = [qs[s, rows, :] for s in seqs]
        kc = [ks[s, rows, :] for s in seqs]
        bt = [bx[s, rows, :] for s in seqs]
        gcum = [_dot_exact_lhs(l_incl, gx[s, rows, :]) for s in seqs]
        gdiff = [_dot_exact_lhs(l_incl, jnp.where(strict, gpk[s, rows, :], 0.0)) for s in seqs]
        decay = [jnp.where(incl, jnp.exp(gdiff[s]), 0.0) for s in seqs]
        e_g = [jnp.exp(gcum[s]) for s in seqs]
        g_last = [gcum[s][chunk - 1:chunk, :] for s in seqs]
        e_last = [jnp.exp(g_last[s]) for s in seqs]
        kdec = [kc[s] * jnp.exp(g_last[s] - gcum[s]) for s in seqs]
        rhs_v = [bt[s] * vs[s, rows, :] for s in seqs]
        rhs_k = [bt[s] * e_g[s] * kc[s] for s in seqs]
        qe = [qc[s] * e_g[s] for s in seqs]

        kq = {}
        for s, gi in units:
            k_bf = kc[s][:, gcols(gi)].astype(BF16)
            lhs = jnp.concatenate([kc[s][:, gcols(gi)], qc[s][:, gcols(gi)]], axis=0).astype(BF16)
            kq[s, gi] = lax.dot_general(lhs, blockdiag(k_bf, m_nt), (((1,), (1,)), ((), ())),
                                        preferred_element_type=F32)
        dec_g = {(s, gi): decay[s][:, gi * pw:(gi + 1) * pw] for s, gi in units}
        bpack = [bpk[s, rows, :] for s in seqs]
        a_g = {(s, gi): jnp.where(strict_g, bpack[s][:, gi * pw:(gi + 1) * pw] * kq[s, gi][:chunk] * dec_g[s, gi],
                                  0.0) for s, gi in units}
        tinv = {u: eye - a_g[u] for u in units}
        xs = {u: _split2(a_g[u]) for u in units}
        xbd = {u: (blockdiag(xs[u][0], m_sq), blockdiag(xs[u][1], m_sq)) for u in units}
        for _ in range(n_sq):
            x = {u: dot3(xs[u][0], xs[u][1], xbd[u][0], xbd[u][1]) for u in units}
            xs = {u: _split2(x[u]) for u in units}
            xbd = {u: (blockdiag(xs[u][0], m_sq), blockdiag(xs[u][1], m_sq)) for u in units}
            ts = {u: _split2(tinv[u]) for u in units}
            tinv = {u: tinv[u] + dot3(ts[u][0], ts[u][1], xbd[u][0], xbd[u][1]) for u in units}
        uw = {}
        for s, gi in units:
            rh, rl = _split2(jnp.concatenate([rhs_v[s][:, gcols(gi)], rhs_k[s][:, gcols(gi)]], axis=1))
            th, tl = _split2(tinv[s, gi])
            uw[s, gi] = dot3(th, tl, blockdiag(rh, m_nt), blockdiag(rl, m_nt))
        s_all = {sh: s_scr[sh[0], sh[1]] for sh in sheads}
        wq = {}
        for s, h in sheads:
            gi, hl = divmod(h, hpg)
            w_h = uw[s, gi][:, gw + hl * dk:gw + (hl + 1) * dk]
            wq[s, h] = _bdot(jnp.concatenate([w_h, qe[s][:, hcols(h)]], axis=0), s_all[s, h])
        vn_bf = {}
        for s, gi in units:
            v_new = jnp.concatenate([uw[s, gi][:, hl * dk:(hl + 1) * dk] - wq[s, gi * hpg + hl][:chunk]
                                     for hl in range(hpg)], axis=1)
            vn_bf[s, gi] = v_new.astype(BF16)
        o_intra = {u: jnp.dot((kq[u][chunk:] * dec_g[u]).astype(BF16), blockdiag(vn_bf[u], m_nt),
                              preferred_element_type=F32) for u in units}
        upd = {}
        for s, h in sheads:
            gi, hl = divmod(h, hpg)
            upd[s, h] = _bdot_tn(kdec[s][:, hcols(h)], vn_bf[s, gi][:, hl * dk:(hl + 1) * dk])
        zc = [z_ref[0, s, rows, :] for s in seqs]
        for s, h in sheads:
            gi, hl = divmod(h, hpg)
            s_scr[s, h] = e_last[s][:, hcols(h)] * s_all[s, h] + upd[s, h]
            o_h = wq[s, h][chunk:] + o_intra[s, gi][:, hl * dk:(hl + 1) * dk]
            o_n = o_h * lax.rsqrt(jnp.mean(o_h * o_h, axis=-1, keepdims=True) + EPS) * gn
            o_ref[s, rows, hcols(h)] = o_n * _silu(zc[s][:, hcols(h)])
        return carry

    lax.fori_loop(0, tb // chunk, chunk_body, 0)
    s_out_ref[...] = s_scr[...]


def _gdn(p_all, ab, conv_w3, prm, gn, tail0, s0, *, n_seq, seq, ns, tb, chunk, seq_off, heads):
    planes, rows, hd = p_all.shape
    dk = hd // heads
    nblk = seq // tb
    conv_w = conv_w3.shape[0]
    pk = heads * chunk
    hpg = min(heads, MXU_DIM // chunk)
    assert seq % tb == 0 and tb % chunk == 0 and tb >= SUBLANES and chunk & (chunk - 1) == 0
    assert conv_w - 1 <= SUBLANES and pk % LANES == 0 and n_seq % ns == 0 and rows % seq == 0
    p4 = p_all.reshape(planes, rows // seq, seq, hd)
    ab3 = ab.reshape(rows // seq, seq, LANES)
    boff = seq_off // ns
    assert seq_off % ns == 0

    def plane(p):
        return pl.BlockSpec((1, ns, tb, hd), lambda b, t: (p, boff + b, t, 0))

    kern = functools.partial(_gdn_kernel, chunk=chunk, heads=heads, conv_w=conv_w)
    return pl.pallas_call(
        kern,
        out_shape=(
            jax.ShapeDtypeStruct((n_seq, seq, hd), F32),
            jax.ShapeDtypeStruct((n_seq, heads, dk, dk), F32),
            jax.ShapeDtypeStruct((n_seq, 3, SUBLANES, hd), F32),
        ),
        grid=(n_seq // ns, nblk),
        in_specs=[
            plane(0), plane(1), plane(2), plane(3),
            pl.BlockSpec((ns, tb, LANES), lambda b, t: (boff + b, t, 0)),
            pl.BlockSpec((conv_w, 3, hd), lambda b, t: (0, 0, 0)),
            pl.BlockSpec((SUBLANES, LANES), lambda b, t: (0, 0)),
            pl.BlockSpec((1, dk), lambda b, t: (0, 0)),
            pl.BlockSpec((3, SUBLANES, hd), lambda b, t: (0, 0, 0)),
            pl.BlockSpec((heads, dk, dk), lambda b, t: (0, 0, 0)),
        ],
        out_specs=(
            pl.BlockSpec((ns, tb, hd), lambda b, t: (b, t, 0)),
            pl.BlockSpec((ns, heads, dk, dk), lambda b, t: (b, 0, 0, 0)),
            pl.BlockSpec((ns, 3, SUBLANES, hd), lambda b, t: (b, 0, 0, 0)),
        ),
        scratch_shapes=[
            pltpu.VMEM((ns, 3, tb + SUBLANES, hd), F32),
            pltpu.VMEM((ns, tb, hd), F32), pltpu.VMEM((ns, tb, hd), F32), pltpu.VMEM((ns, tb, hd), F32),
            pltpu.VMEM((ns, tb, hd), F32), pltpu.VMEM((ns, tb, hd), F32),
            pltpu.VMEM((ns, tb, pk), F32), pltpu.VMEM((ns, tb, pk), F32),
            pltpu.VMEM((ns, heads, dk, dk), F32),
            pltpu.VMEM((hpg * chunk, hpg * chunk), BF16), pltpu.VMEM((hpg * chunk, 2 * hpg * dk), BF16),
        ],
        compiler_params=_cparams(("parallel", "arbitrary")),
        name=f"gdn_chunk{chunk}",
    )(p4, p4, p4, p4, ab3, conv_w3, prm, gn, tail0, s0)


def _sb_prompt_kernel(bias_ref, q_ref, k_ref, v_ref, km_ref, vm_ref, o_ref, kbf, vtb, *, scale, tq, dh):
    seq = k_ref.shape[1]
    hps = k_ref.shape[2] // dh
    n_meta = km_ref.shape[0]
    nkb = seq // tq
    hg = pl.program_id(1)
    qi = pl.program_id(2)
    hx = range(hps)

    def hsl(x):
        return slice(x * dh, (x + 1) * dh)

    @pl.when(qi == 0)
    def _():
        for x in hx:
            for j in range(nkb):
                kbf[x, j] = k_ref[0, j * tq:(j + 1) * tq, hsl(x)].astype(BF16)
                vtb[x, j] = v_ref[0, j * tq:(j + 1) * tq, hsl(x)].T.astype(BF16)

    bias = [bias_ref[hg * hps + x] for x in hx]
    q_all = q_ref[0] * scale
    qt = [q_all[:, hsl(x)].T.astype(BF16) for x in hx]

    def suffix_mat(n):
        r = lax.broadcasted_iota(jnp.int32, (n, n), 0)
        c = lax.broadcasted_iota(jnp.int32, (n, n), 1)
        return jnp.where(c > r, 1.0, 0.0).astype(BF16)

    u_full = suffix_mat(tq)

    def logits(tiles):
        zt = [jnp.dot(k_bf, qt[x], preferred_element_type=F32) + bias[x] for x, k_bf, _, _ in tiles]
        sp = [_softplus(z) for z in zt]
        ls = [s if t[3] is None else jnp.where(t[3], s, 0.0) for s, t in zip(sp, tiles)]
        parts = [_split2(v) for v in ls]
        after = [jnp.dot(t[2], p[0], preferred_element_type=F32) for t, p in zip(tiles, parts)]
        after = [a + jnp.dot(t[2], p[1], preferred_element_type=F32) for a, t, p in zip(after, tiles, parts)]
        return [(t[0], z - s, l, a, t[3]) for z, s, l, a, t in zip(zt, sp, ls, after, tiles)]

    def attend(pre, vts, state):
        carry = list(state[:hps])
        acc = list(state[hps:])
        a_bf = []
        for (x, lz, ls, after, mask) in pre:
            a = jnp.exp(lz - after - carry[x])
            if mask is not None:
                a = jnp.where(mask, a, 0.0)
            a_bf.append(a.astype(BF16))
            carry[x] = carry[x] + after[0:1, :] + ls[0:1, :]
        for (x, _, _, _, _), a, vt_bf in zip(pre, a_bf, vts):
            acc[x] = acc[x] + jnp.dot(vt_bf, a, preferred_element_type=F32)
        return tuple(carry) + tuple(acc)

    r = lax.broadcasted_iota(jnp.int32, (tq, tq), 0)
    c = lax.broadcasted_iota(jnp.int32, (tq, tq), 1)
    u_meta = suffix_mat(n_meta)
    km = km_ref[...].astype(BF16)
    pre = logits([(x, kbf[x, qi], u_full, r < c) for x in hx] + [(x, km[:, hsl(x)], u_meta, None) for x in hx])
    pre_diag, pre_meta = pre[:hps], pre[hps:]
    state = tuple(jnp.zeros((1, tq), F32) for _ in hx) + tuple(jnp.zeros((dh, tq), F32) for _ in hx)
    state = attend(pre_diag, [vtb[x, qi] for x in hx], state)

    def pair(i, st):
        kb = qi - 1 - 2 * i
        tiles = [(x, kbf[x, kb - d], u_full, None) for d in range(2) for x in hx]
        return attend(logits(tiles), [vtb[x, kb - d] for d in range(2) for x in hx], st)

    state = lax.fori_loop(0, qi // 2, pair, state)

    def last_single(st):
        return attend(logits([(x, kbf[x, 0], u_full, None) for x in hx]), [vtb[x, 0] for x in hx], st)

    state = lax.cond(qi % 2 == 1, last_single, lambda st: st, state)
    vm = vm_ref[...]
    state = attend(pre_meta, [vm[:, hsl(x)].T.astype(BF16) for x in hx], state)
    for x in hx:
        o_ref[:, hsl(x)] = state[hps + x].T


def _sb_prompt(p_all, k_meta, v_meta, bias, *, n_seq, seq, heads, q_plane, tq, hps):
    hd = p_all.shape[2]
    dh = hd // heads
    nq = seq // tq
    n_meta = k_meta.shape[0]
    hw = hps * dh
    assert seq % tq == 0 and heads % hps == 0
    kern = functools.partial(_sb_prompt_kernel, scale=dh ** -0.5, tq=tq, dh=dh)
    return pl.pallas_call(
        kern,
        out_shape=jax.ShapeDtypeStruct((n_seq * seq, hd), F32),
        grid=(n_seq, heads // hps, nq),
        in_specs=[
            pl.BlockSpec(memory_space=pltpu.SMEM),
            pl.BlockSpec((1, tq, hw), lambda b, h, i: (q_plane, b * nq + i, h)),
            pl.BlockSpec((1, seq, hw), lambda b, h, i: (q_plane + 1, b, h)),
            pl.BlockSpec((1, seq, hw), lambda b, h, i: (q_plane + 2, b, h)),
            pl.BlockSpec((n_meta, hw), lambda b, h, i: (0, h)),
            pl.BlockSpec((n_meta, hw), lambda b, h, i: (0, h)),
        ],
        out_specs=pl.BlockSpec((tq, hw), lambda b, h, i: (b * nq + i, h)),
        scratch_shapes=[pltpu.VMEM((hps, nq, tq, dh), BF16), pltpu.VMEM((hps, nq, dh, tq), BF16)],
        compiler_params=_cparams(("parallel", "parallel", "arbitrary")),
        name="sb_prompt",
    )(bias, p_all, p_all, p_all, k_meta, v_meta)


def _merge_kernel(oa_ref, ob_ref, ga_ref, gb_ref, x_ref, wpa_ref, wpb_ref, wo_ref, out_ref):
    pa = jnp.dot(oa_ref[...].astype(BF16), wpa_ref[...], preferred_element_type=F32)
    pb = jnp.dot(ob_ref[...].astype(BF16), wpb_ref[...], preferred_element_type=F32)
    m = _sigmoid(ga_ref[0]) * pa + _sigmoid(gb_ref[0]) * pb
    out_ref[...] = x_ref[...] + jnp.dot(m.astype(BF16), wo_ref[...], preferred_element_type=F32)


def _merge(o_a, o_b, p_all, x, w_pa, w_pb, w_o, *, gate_plane, tm):
    m, d = x.shape
    da = o_a.shape[1]
    db = o_b.shape[1]
    assert m % tm == 0

    def whole(shape):
        return pl.BlockSpec(shape, lambda i: (0, 0))

    return pl.pallas_call(
        _merge_kernel,
        out_shape=jax.ShapeDtypeStruct((m, d), F32),
        grid=(m // tm,),
        in_specs=[
            pl.BlockSpec((tm, da), lambda i: (i, 0)),
            pl.BlockSpec((tm, db), lambda i: (i, 0)),
            pl.BlockSpec((1, tm, d), lambda i: (gate_plane, i, 0)),
            pl.BlockSpec((1, tm, d), lambda i: (gate_plane + 1, i, 0)),
            pl.BlockSpec((tm, d), lambda i: (i, 0)),
            whole(w_pa.shape), whole(w_pb.shape), whole(w_o.shape),
        ],
        out_specs=pl.BlockSpec((tm, d), lambda i: (i, 0)),
        compiler_params=_cparams(("parallel",)),
        name="merge",
    )(o_a, o_b, p_all, p_all, x, w_pa, w_pb, w_o)


def _ffn_kernel(h_ref, nw_ref, wg_ref, wu_ref, wo_ref, nf_ref, y_ref, xn_ref, acc_ref):
    f = pl.program_id(1)

    @pl.when(f == 0)
    def _():
        h = h_ref[...]
        xn_ref[...] = _rms(h, nw_ref[...]).astype(BF16)
        acc_ref[...] = h

    xn = xn_ref[...]
    gt = jnp.dot(xn, wg_ref[...], preferred_element_type=F32)
    up = jnp.dot(xn, wu_ref[...], preferred_element_type=F32)
    acc_ref[...] += jnp.dot((_silu(gt) * up).astype(BF16), wo_ref[...], preferred_element_type=F32)

    @pl.when(f == pl.num_programs(1) - 1)
    def _():
        y_ref[...] = _rms(acc_ref[...], nf_ref[...])


def _ffn(h, norm_w, w_in, w_out, norm_f, *, tm, tf):
    m, d = h.shape
    dff = w_out.shape[0]
    nf = dff // tf
    assert m % tm == 0 and dff % tf == 0
    return pl.pallas_call(
        _ffn_kernel,
        out_shape=jax.ShapeDtypeStruct((m, d), F32),
        grid=(m // tm, nf),
        in_specs=[
            pl.BlockSpec((tm, d), lambda i, f: (i, 0)),
            pl.BlockSpec((1, d), lambda i, f: (0, 0)),
            pl.BlockSpec((d, tf), lambda i, f: (0, f)),
            pl.BlockSpec((d, tf), lambda i, f: (0, f + nf)),
            pl.BlockSpec((tf, d), lambda i, f: (f, 0)),
            pl.BlockSpec((1, d), lambda i, f: (0, 0)),
        ],
        out_specs=pl.BlockSpec((tm, d), lambda i, f: (i, 0)),
        scratch_shapes=[pltpu.VMEM((tm, d), BF16), pltpu.VMEM((tm, d), F32)],
        compiler_params=_cparams(("parallel", "arbitrary")),
        name="ffn",
    )(h, norm_w, w_in, w_in, w_out, norm_f)


def _gdn_step_kernel(q_ref, k_ref, v_ref, z_ref, ab_ref, cw_ref, prm_ref, gn_ref, buf_ref, s_ref,
                     o_ref, s_out_ref, qs, ks, vs, gx, bx, *, heads, conv_w):
    nb = qs.shape[0]
    hd = qs.shape[1]
    dk = hd // heads
    b = pl.program_id(0)

    @pl.when(b == 0)
    def _():
        srcs = (q_ref, k_ref, v_ref)
        dsts = (qs, ks, vs)
        for p in range(3):
            y = cw_ref[conv_w - 1, p:p + 1, :] * srcs[p][0, 0:nb, :]
            for i in range(conv_w - 1):
                y = y + cw_ref[i, p:p + 1, :] * buf_ref[i, p]
            c = _silu(y)
            if p == 2:
                dsts[p][...] = c
            else:
                scale = dk ** -0.5 if p == 0 else 1.0
                for h in range(heads):
                    ch = c[:, h * dk:(h + 1) * dk]
                    inv = lax.rsqrt(jnp.sum(ch * ch, axis=-1, keepdims=True) + EPS)
                    dsts[p][:, h * dk:(h + 1) * dk] = ch * (inv * scale)
        ab = ab_ref[0:nb, :]
        g_t = -jnp.exp(prm_ref[0:1, :]) * _softplus(ab + prm_ref[1:2, :])
        b_t = _sigmoid(ab)
        gx[...] = _dot_exact_rhs(g_t, _head_select(hd, dk))
        bx[...] = _dot_exact_rhs(b_t, _head_select(hd, dk, heads))

    rb = pl.ds(b, 1)
    qrow = qs[rb, :]
    krow = ks[rb, :]
    vrow = vs[rb, :]
    e_g = jnp.exp(gx[rb, :])
    beta = bx[rb, :]
    zrow = z_ref[0, rb, :]
    gn = gn_ref[...]
    first = lax.broadcasted_iota(jnp.int32, (SUBLANES, dk), 0) == 0
    for h in range(heads):
        sl = slice(h * dk, (h + 1) * dk)
        s = s_ref[0, h]
        kh = krow[:, sl]
        qh = qrow[:, sl]
        lhs = jnp.where(first, kh, qh)
        prod = _dot_hi(jnp.broadcast_to(lhs, (SUBLANES, dk)), s)
        k_s = prod[0:1, :]
        q_s = prod[1:2, :]
        v_new = beta[:, sl] * (vrow[:, sl] - e_g[:, sl] * k_s)
        qk = jnp.sum(qh * kh, axis=-1, keepdims=True)
        o_h = e_g[:, sl] * q_s + qk * v_new
        k8 = jnp.where(first, kh, 0.0)
        v8 = jnp.where(first, v_new, 0.0)
        kh_hi, kh_lo = _split2(k8)
        vn_hi, vn_lo = _split2(v8)
        outer = (_bdot_tn(kh_hi, vn_hi) + _bdot_tn(kh_hi, vn_lo)) + _bdot_tn(kh_lo, vn_hi)
        s_out_ref[0, h] = e_g[:, sl] * s + outer
        o_n = o_h * lax.rsqrt(jnp.mean(o_h * o_h, axis=-1, keepdims=True) + EPS) * gn
        o_ref[0, :, sl] = o_n * _silu(zrow[:, sl])


def _gdn_step(p_small, ab, conv_w3, prm, gn, buf, state, *, heads):
    nb = state.shape[0]
    ms, hd = p_small.shape[1], p_small.shape[2]
    dk = hd // heads
    conv_w = conv_w3.shape[0]
    kern = functools.partial(_gdn_step_kernel, heads=heads, conv_w=conv_w)

    def plane(p):
        return pl.BlockSpec((1, ms, hd), lambda b: (p, 0, 0))

    return pl.pallas_call(
        kern,
        out_shape=(jax.ShapeDtypeStruct((nb, 1, hd), F32), jax.ShapeDtypeStruct(state.shape, F32)),
        grid=(nb,),
        in_specs=[
            plane(0), plane(1), plane(2), plane(3),
            pl.BlockSpec((ms, LANES), lambda b: (0, 0)),
            pl.BlockSpec((conv_w, 3, hd), lambda b: (0, 0, 0)),
            pl.BlockSpec((SUBLANES, LANES), lambda b: (0, 0)),
            pl.BlockSpec((1, dk), lambda b: (0, 0)),
            pl.BlockSpec((conv_w - 1, 3, nb, hd), lambda b: (0, 0, 0, 0)),
            pl.BlockSpec((1, heads, dk, dk), lambda b: (b, 0, 0, 0)),
        ],
        out_specs=(
            pl.BlockSpec((1, 1, hd), lambda b: (b, 0, 0)),
            pl.BlockSpec((1, heads, dk, dk), lambda b: (b, 0, 0, 0)),
        ),
        scratch_shapes=[pltpu.VMEM((nb, hd), F32) for _ in range(5)],
        compiler_params=_cparams(("arbitrary",)),
        name="gdn_step",
    )(p_small, p_small, p_small, p_small, ab, conv_w3, prm, gn, buf, state)


def _sb_step_kernel(pt_ref, q_ref, kn_ref, vn_ref, bias_ref, *refs, scale, pages_per_step, heads, past_len):
    k_refs = refs[:pages_per_step]
    v_refs = refs[pages_per_step:2 * pages_per_step]
    o_ref = refs[2 * pages_per_step]
    qbd, acc, carry = refs[2 * pages_per_step + 1:]
    dh = k_refs[0].shape[2]
    page = k_refs[0].shape[1] // heads
    hd = heads * dh
    b = pl.program_id(0)
    j = pl.program_id(1)
    lane_head = lax.broadcasted_iota(jnp.int32, (dh, LANES), 1)

    @pl.when(j == 0)
    def _():
        qrow = q_ref[pl.ds(b, 1), :] * scale
        for h in range(heads):
            qh_t = jnp.broadcast_to(qrow[:, h * dh:(h + 1) * dh], (LANES, dh)).T
            qbd[h * dh:(h + 1) * dh, :] = jnp.where(lane_head == h, qh_t, 0.0).astype(BF16)
        acc[...] = jnp.zeros_like(acc)
        carry[...] = jnp.zeros_like(carry)

    def head_major(ref):
        return jnp.concatenate([ref[0, pl.ds(h, page, stride=heads), :] for h in range(heads)], axis=1)

    r = lax.broadcasted_iota(jnp.int32, (page, page), 0)
    c = lax.broadcasted_iota(jnp.int32, (page, page), 1)
    u_bf = jnp.where(c > r, 1.0, 0.0).astype(BF16)
    expand = _head_select(hd, dh)
    bias = bias_ref[...]
    n = pages_per_step
    z = [jnp.dot(head_major(k_refs[i]).astype(BF16), qbd[...], preferred_element_type=F32) + bias
         for i in range(n)]
    sp = [_softplus(z[i]) for i in range(n)]
    parts = [_split2(sp[i]) for i in range(n)]
    after = [jnp.dot(u_bf, parts[i][0], preferred_element_type=F32) for i in range(n)]
    after = [after[i] + jnp.dot(u_bf, parts[i][1], preferred_element_type=F32) for i in range(n)]
    cur = carry[...]
    a = []
    for i in range(n):
        a.append(jnp.exp(z[i] - sp[i] - after[i] - cur).astype(BF16))
        cur = cur + after[i][0:1, :] + sp[i][0:1, :]
    carry[...] = cur
    a_x = [jnp.dot(a[i], expand, preferred_element_type=F32) for i in range(n)]
    contrib = a_x[0] * head_major(v_refs[0])
    for i in range(1, n):
        contrib = contrib + a_x[i] * head_major(v_refs[i])
    acc[...] += contrib

    @pl.when(j == pl.num_programs(1) - 1)
    def _():
        out = jnp.sum(acc[...], axis=0, keepdims=True)
        qrow = q_ref[pl.ds(b, 1), :] * scale
        krow = kn_ref[pl.ds(b, 1), :]
        vrow = vn_ref[pl.ds(b, 1), :]
        self_valid = past_len < past_len
        for h in range(heads):
            sl = slice(h * dh, (h + 1) * dh)
            z_new = jnp.sum(qrow[:, sl] * krow[:, sl], axis=-1, keepdims=True) + bias[:, h:h + 1]
            a_new = jnp.where(self_valid, jnp.exp(z_new - _softplus(z_new)), 0.0)
            o_ref[0, :, sl] = out[:, sl] + a_new * vrow[:, sl]


def _sb_step(page_table, q, k_new, v_new, bias_row, cache_k, cache_v, *, heads, pages_per_step):
    nb, n_pages = page_table.shape
    rows, dh = cache_k.shape[1], cache_k.shape[2]
    page = rows // heads
    hd = heads * dh
    nsteps = n_pages // pages_per_step
    assert n_pages % pages_per_step == 0
    kern = functools.partial(_sb_step_kernel, scale=dh ** -0.5, pages_per_step=pages_per_step,
                             heads=heads, past_len=n_pages * page)

    def page_spec(i):
        return pl.BlockSpec((1, rows, dh), lambda b, j, pt: (pt[b, n_pages - 1 - (j * pages_per_step + i)], 0, 0))

    def whole(shape):
        return pl.BlockSpec(shape, lambda b, j, pt: (0, 0))

    grid_spec = pltpu.PrefetchScalarGridSpec(
        num_scalar_prefetch=1,
        grid=(nb, nsteps),
        in_specs=[whole(q.shape), whole(k_new.shape), whole(v_new.shape), whole(bias_row.shape)]
        + [page_spec(i) for i in range(pages_per_step)] * 2,
        out_specs=pl.BlockSpec((1, 1, hd), lambda b, j, pt: (b, 0, 0)),
        scratch_shapes=[pltpu.VMEM((hd, LANES), BF16), pltpu.VMEM((page, hd), F32), pltpu.VMEM((1, LANES), F32)],
    )
    out = pl.pallas_call(
        kern,
        out_shape=jax.ShapeDtypeStruct((nb, 1, hd), F32),
        grid_spec=grid_spec,
        compiler_params=_cparams(("parallel", "arbitrary")),
        name="sb_step",
    )(page_table, q, k_new, v_new, bias_row, *([cache_k] * pages_per_step), *([cache_v] * pages_per_step))
    return out.reshape(nb, hd)


def _layer(l, x_prompt, x_sample, cache_k, cache_v, state_rec, state_conv, page_table, meta_tokens,
           norm_mix, w_in, conv_w, a_log, dt_bias, gdn_norm, sb_bias, w_pa, w_pb, w_o,
           norm_ffn, w_ffn_in, w_ffn_out, norm_final):
    n_seq, seq, d = x_prompt.shape
    nb = x_sample.shape[0]
    n_meta = meta_tokens.shape[0]
    heads_a = a_log.shape[1]
    heads_b = sb_bias.shape[1]
    d_qkv = conv_w.shape[2]
    d_a = d_qkv // 3
    d_b = w_pb.shape[1]
    conv_width = conv_w.shape[1]
    dk = d_a // heads_a
    assert d_a == d and d_b == d and 2 * heads_a <= LANES and nb % n_meta == 0

    w = w_in[l]
    c_ab = d_qkv + d_a
    w_main = jnp.concatenate([w[:, :c_ab], w[:, c_ab + 2 * heads_a:]], axis=1).astype(BF16)
    w_ab = jnp.pad(w[:, c_ab:c_ab + 2 * heads_a], ((0, 0), (0, LANES - 2 * heads_a))).astype(BF16)
    nm = norm_mix[l].reshape(1, d)
    conv_w3 = conv_w[l].reshape(conv_width, 3, d_a)
    prm = jnp.zeros((SUBLANES, LANES), F32).at[0, :heads_a].set(a_log[l]).at[1, :heads_a].set(dt_bias[l])
    gn = gdn_norm[l].reshape(1, dk)
    bias = sb_bias[l]
    wpa, wpb, wo = w_pa[l].astype(BF16), w_pb[l].astype(BF16), w_o[l].astype(BF16)
    wfi, wfo = w_ffn_in[l].astype(BF16), w_ffn_out[l].astype(BF16)
    nffn = norm_ffn[l].reshape(1, d)
    nfin = norm_final.reshape(1, d)

    ms = -(-(nb + n_meta) // 64) * 64
    x_small = jnp.concatenate([x_sample.reshape(nb, d), meta_tokens.astype(F32),
                               jnp.zeros((ms - nb - n_meta, d), F32)], axis=0)
    p_small, ab_small = _inproj(x_small, nm, w_main, w_ab, tm=ms)

    zeros_tail = jnp.zeros((3, SUBLANES, d_a), F32)
    zeros_state = jnp.zeros((heads_a, dk, dk), F32)
    _, s_meta, tail_meta = _gdn(p_small, ab_small, conv_w3, prm, gn, zeros_tail, zeros_state,
                                n_seq=1, seq=n_meta, ns=1, tb=n_meta, chunk=n_meta,
                                seq_off=nb // n_meta, heads=heads_a)
    k_meta = p_small[5, nb:nb + n_meta]
    v_meta = p_small[6, nb:nb + n_meta]

    xp = x_prompt.reshape(n_seq * seq, d)
    p_all, ab_all = _inproj(xp, nm, w_main, w_ab, tm=1024)
    o_a, s_prompt, tail_prompt = _gdn(p_all, ab_all, conv_w3, prm, gn, tail_meta[0], s_meta[0],
                                      n_seq=n_seq, seq=seq, ns=2 if n_seq % 2 == 0 else 1, tb=256, chunk=GDN_CHUNK,
                                      seq_off=0, heads=heads_a)
    o_a = o_a.reshape(n_seq * seq, d_a)
    o_b = _sb_prompt(p_all, k_meta, v_meta, bias, n_seq=n_seq, seq=seq, heads=heads_b, q_plane=4, tq=256, hps=2)
    h1 = _merge(o_a, o_b, p_all, xp, wpa, wpb, wo, gate_plane=7, tm=512)
    y_prompt = _ffn(h1, nffn, wfi, wfo, nfin, tm=512, tf=1408).reshape(n_seq, seq, d)

    dh = d_b // heads_b
    k_prompt = jnp.concatenate([jnp.broadcast_to(k_meta[None], (n_seq, n_meta, d_b)),
                                p_all[5].reshape(n_seq, seq, d_b)], axis=1).reshape(n_seq, n_meta + seq, heads_b, dh)
    v_prompt = jnp.concatenate([jnp.broadcast_to(v_meta[None], (n_seq, n_meta, d_b)),
                                p_all[6].reshape(n_seq, seq, d_b)], axis=1).reshape(n_seq, n_meta + seq, heads_b, dh)
    conv_prompt = jnp.transpose(tail_prompt[:, :, SUBLANES - (conv_width - 1):, :], (0, 2, 1, 3)).reshape(
        n_seq, conv_width - 1, d_qkv)

    buf = jnp.transpose(state_conv[l].reshape(nb, conv_width - 1, 3, d_a), (1, 2, 0, 3))
    o_a_s, s_sample = _gdn_step(p_small, ab_small, conv_w3, prm, gn, buf, state_rec[l], heads=heads_a)
    o_a_s = o_a_s.reshape(nb, d_a)
    n_phys, page = cache_k.shape[1], cache_k.shape[2]
    bias_row = jnp.zeros((1, LANES), F32).at[0, :heads_b].set(bias)
    o_b_s = _sb_step(page_table, p_small[4, :nb], p_small[5, :nb], p_small[6, :nb], bias_row,
                     cache_k[l].reshape(n_phys, page * heads_b, dh), cache_v[l].reshape(n_phys, page * heads_b, dh),
                     heads=heads_b, pages_per_step=4)
    h1_s = _merge(o_a_s, o_b_s, p_small, x_small[:nb], wpa, wpb, wo, gate_plane=7, tm=nb)
    y_sample = _ffn(h1_s, nffn, wfi, wfo, nfin, tm=nb, tf=1408).reshape(nb, 1, d)

    k_sample = p_small[5, :nb].reshape(nb, 1, heads_b, dh)
    v_sample = p_small[6, :nb].reshape(nb, 1, heads_b, dh)
    qkv_new = jnp.transpose(p_small[0:3, :nb], (1, 0, 2)).reshape(nb, 1, d_qkv)
    conv_sample = jnp.concatenate([state_conv[l][:, 1:], qkv_new], axis=1)
    return (y_prompt, y_sample, k_prompt, v_prompt, k_sample, v_sample,
            s_prompt, s_sample, conv_prompt, conv_sample)


def kernel(x_prompt, x_sample, cache_k, cache_v, state_rec, state_conv, page_table, meta_tokens, norm_mix, w_in, conv_w, a_log, dt_bias, gdn_norm, sb_bias, w_pa, w_pb, w_o, norm_ffn, w_ffn_in, w_ffn_out, norm_final):
    depth = w_in.shape[0]
    assert depth == 1 and x_sample.shape[1] == 1, "single layer, single decode token per sequence"
    outs = _layer(0, x_prompt, x_sample, cache_k, cache_v, state_rec, state_conv, page_table, meta_tokens,
                  norm_mix, w_in, conv_w, a_log, dt_bias, gdn_norm, sb_bias, w_pa, w_pb, w_o,
                  norm_ffn, w_ffn_in, w_ffn_out, norm_final)
    (y_prompt, y_sample, k_p, v_p, k_s, v_s, rec_p, rec_s, conv_p, conv_s) = outs
    return (y_prompt, y_sample, k_p[None], v_p[None], k_s[None], v_s[None],
            rec_p[None], rec_s[None], conv_p[None], conv_s[None])
```

```python
import functools

import jax
import jax.numpy as jnp
from jax import lax
from jax.experimental import pallas as pl
from jax.experimental.pallas import tpu as pltpu

F32 = jnp.float32
BF16 = jnp.bfloat16
EPS = 1e-6
LANES = 128
SUBLANES = 8
MXU_DIM = 256
GDN_CHUNK = 64
VMEM_LIMIT = 56 * 1024 * 1024


def _cparams(sem):
    return pltpu.CompilerParams(dimension_semantics=sem, vmem_limit_bytes=VMEM_LIMIT)


def _bdot(a, b):
    return jnp.dot(a.astype(BF16), b.astype(BF16), preferred_element_type=F32)


def _bdot_nt(a, b):
    return lax.dot_general(a.astype(BF16), b.astype(BF16), (((1,), (1,)), ((), ())),
                           preferred_element_type=F32)


def _bdot_tn(a, b):
    return lax.dot_general(a.astype(BF16), b.astype(BF16), (((0,), (0,)), ((), ())),
                           preferred_element_type=F32)


def _split2(x):
    hi = x.astype(BF16)
    lo = (x - hi.astype(F32)).astype(BF16)
    return hi, lo


def _split3(x):
    hi = x.astype(BF16)
    r = x - hi.astype(F32)
    mid = r.astype(BF16)
    lo = (r - mid.astype(F32)).astype(BF16)
    return hi, mid, lo


def _dot_exact_lhs(lhs_bf, x, passes=3):
    parts = _split3(x) if passes == 3 else _split2(x)
    out = jnp.dot(lhs_bf, parts[0], preferred_element_type=F32)
    for p in parts[1:]:
        out = out + jnp.dot(lhs_bf, p, preferred_element_type=F32)
    return out


def _dot_exact_rhs(x, rhs_bf, passes=3):
    parts = _split3(x) if passes == 3 else _split2(x)
    out = jnp.dot(parts[0], rhs_bf, preferred_element_type=F32)
    for p in parts[1:]:
        out = out + jnp.dot(p, rhs_bf, preferred_element_type=F32)
    return out


def _dot_hi(a, b):
    ah, al = _split2(a)
    bh, bl = _split2(b)
    out = jnp.dot(ah, bh, preferred_element_type=F32)
    out = out + jnp.dot(ah, bl, preferred_element_type=F32)
    return out + jnp.dot(al, bh, preferred_element_type=F32)


def _softplus(x):
    return jnp.maximum(x, 0.0) + jnp.log(1.0 + jnp.exp(-jnp.abs(x)))


def _sigmoid(x):
    return 1.0 / (1.0 + jnp.exp(-x))


def _silu(x):
    return x * _sigmoid(x)


def _rms(x, w):
    return x * lax.rsqrt(jnp.mean(x * x, axis=-1, keepdims=True) + EPS) * w


def _head_select(width, head_dim, lane_offset=0):
    shift = head_dim.bit_length() - 1
    assert 1 << shift == head_dim
    row = lax.broadcasted_iota(jnp.int32, (LANES, width), 0)
    col_head = lax.shift_right_logical(lax.broadcasted_iota(jnp.int32, (LANES, width), 1), shift)
    return jnp.where(row == col_head + lane_offset, 1.0, 0.0).astype(BF16)


def _inproj_kernel(x_ref, nw_ref, w_ref, wab_ref, out_ref, ab_ref, k_out_ref, v_out_ref, xn_ref):
    j = pl.program_id(1)
    n_planes = pl.num_programs(1) - 2

    @pl.when(j == 0)
    def _():
        xn_ref[...] = _rms(x_ref[...], nw_ref[...]).astype(BF16)
        ab_ref[...] = jnp.dot(xn_ref[...], wab_ref[...], preferred_element_type=F32)

    res = jnp.dot(xn_ref[...], w_ref[...], preferred_element_type=F32)

    @pl.when(j < n_planes)
    def _():
        out_ref[0] = res

    @pl.when(j == n_planes)
    def _():
        k_out_ref[0] = res

    @pl.when(j == n_planes + 1)
    def _():
        v_out_ref[0] = res


def _inproj(x, norm_w, w_main, w_ab, tm, *, seq, row_off):
    m, d = x.shape
    n = w_main.shape[1]
    tn = d
    nj = n // tn
    tps = seq // tm
    assert m % seq == 0 and seq % tm == 0 and n % tn == 0 and row_off % SUBLANES == 0 and tm % SUBLANES == 0
    kv_shape = jax.ShapeDtypeStruct((m // seq, row_off + seq, tn), F32)
    kv_spec = pl.BlockSpec((pl.Element(1), pl.Element(tm), pl.Element(tn)),
                           lambda i, j: (i // tps, pl.multiple_of(row_off + (i % tps) * tm, SUBLANES), 0))
    return pl.pallas_call(
        _inproj_kernel,
        out_shape=(jax.ShapeDtypeStruct((nj - 2, m, tn), F32), jax.ShapeDtypeStruct((m, LANES), F32),
                   kv_shape, kv_shape),
        grid=(m // tm, nj),
        in_specs=[
            pl.BlockSpec((tm, d), lambda i, j: (i, 0)),
            pl.BlockSpec((1, d), lambda i, j: (0, 0)),
            pl.BlockSpec((d, tn), lambda i, j: (0, j)),
            pl.BlockSpec((d, LANES), lambda i, j: (0, 0)),
        ],
        out_specs=(
            pl.BlockSpec((1, tm, tn), lambda i, j: (jnp.minimum(j, nj - 3), i, 0)),
            pl.BlockSpec((tm, LANES), lambda i, j: (i, 0)),
            kv_spec, kv_spec,
        ),
        scratch_shapes=[pltpu.VMEM((tm, d), BF16)],
        compiler_params=_cparams(("parallel", "arbitrary")),
        name="inproj",
    )(x, norm_w, w_main, w_ab)


def _fill_rows_kernel(rows_ref, buf_ref, out_ref):
    del buf_ref
    out_ref[0] = rows_ref[...]


def _fill_rows(buf, rows):
    n, _, d = buf.shape
    r = rows.shape[0]
    assert r % SUBLANES == 0
    return pl.pallas_call(
        _fill_rows_kernel,
        out_shape=jax.ShapeDtypeStruct(buf.shape, buf.dtype),
        grid=(n,),
        in_specs=[pl.BlockSpec((r, d), lambda b: (0, 0)), pl.BlockSpec(memory_space=pl.ANY)],
        out_specs=pl.BlockSpec((1, r, d), lambda b: (b, 0, 0)),
        input_output_aliases={1: 0},
        compiler_params=_cparams(("parallel",)),
        name="fill_meta_rows",
    )(rows, buf)


def _gdn_kernel(q_ref, k_ref, v_ref, z_ref, ab_ref, cw_ref, prm_ref, gn_ref, tail0_ref, s0_ref,
                o_ref, s_out_ref, tail_out_ref,
                xpad, qs, ks, vs, gx, bx, gpk, bpk, s_scr, m_sq, m_nt, *, chunk, heads, conv_w):
    ns = q_ref.shape[1]
    tb = q_ref.shape[2]
    hd = q_ref.shape[3]
    dk = hd // heads
    pk = heads * chunk
    t = pl.program_id(1)

    @pl.when(t == 0)
    def _():
        for s in range(ns):
            s_scr[s] = s0_ref[...]
            xpad[s, :, 0:SUBLANES, :] = tail0_ref[...]

    srcs = (q_ref, k_ref, v_ref)
    dsts = (qs, ks, vs)
    sel_g, sel_b = _head_select(hd, dk), _head_select(hd, dk, heads)
    sel_gp, sel_bp = _head_select(pk, chunk), _head_select(pk, chunk, heads)
    for s in range(ns):
        for p in range(3):
            xpad[s, p, SUBLANES:SUBLANES + tb, :] = srcs[p][0, s]
        for p in range(3):
            y = None
            for i in range(conv_w):
                term = cw_ref[i, p:p + 1, :] * xpad[s, p, pl.ds(SUBLANES - (conv_w - 1) + i, tb), :]
                y = term if y is None else y + term
            c = _silu(y)
            if p == 2:
                dsts[p][s] = c
            else:
                scale = dk ** -0.5 if p == 0 else 1.0
                for h in range(heads):
                    ch = c[:, h * dk:(h + 1) * dk]
                    inv = lax.rsqrt(jnp.sum(ch * ch, axis=-1, keepdims=True) + EPS)
                    dsts[p][s, :, h * dk:(h + 1) * dk] = ch * (inv * scale)
        new_tail = xpad[s, :, tb:tb + SUBLANES, :]
        xpad[s, :, 0:SUBLANES, :] = new_tail
        tail_out_ref[s] = new_tail

        ab = ab_ref[s]
        g_t = -jnp.exp(prm_ref[0:1, :]) * _softplus(ab + prm_ref[1:2, :])
        b_t = _sigmoid(ab)
        gx[s] = _dot_exact_rhs(g_t, sel_g)
        bx[s] = _dot_exact_rhs(b_t, sel_b)
        gpk[s] = _dot_exact_rhs(g_t, sel_gp)
        bpk[s] = _dot_exact_rhs(b_t, sel_bp)

    hpg = min(heads, MXU_DIM // chunk)
    ngrp = heads // hpg
    pw = hpg * chunk
    gw = hpg * dk
    sh_c = chunk.bit_length() - 1
    sh_d = dk.bit_length() - 1

    def iota(shape, dim):
        return lax.broadcasted_iota(jnp.int32, shape, dim)

    ri = iota((chunk, pk), 0)
    ci = iota((chunk, pk), 1) & (chunk - 1)
    incl = ri >= ci
    strict = ri > ci
    eye = jnp.where(ri == ci, 1.0, 0.0)[:, :pw]
    strict_g = strict[:, :pw]
    l_incl = jnp.where(iota((chunk, chunk), 0) >= iota((chunk, chunk), 1), 1.0, 0.0).astype(BF16)
    @pl.when(t == 0)
    def _():
        m_sq[...] = jnp.where((iota((pw, pw), 0) >> sh_c) == (iota((pw, pw), 1) >> sh_c), 1.0, 0.0).astype(BF16)
        m_nt[...] = jnp.where((iota((pw, 2 * gw), 0) >> sh_c) == ((iota((pw, 2 * gw), 1) >> sh_d) & (hpg - 1)),
                              1.0, 0.0).astype(BF16)

    n_sq = sh_c - 1
    gn = gn_ref[...]

    def blockdiag(x_bf, mask_ref):
        return jnp.concatenate([x_bf] * hpg, axis=0) * mask_ref[:, :x_bf.shape[1]]

    def dot3(ah, al, bh, bl):
        out = jnp.dot(ah, bh, preferred_element_type=F32)
        out = out + jnp.dot(ah, bl, preferred_element_type=F32)
        return out + jnp.dot(al, bh, preferred_element_type=F32)

    seqs = range(ns)
    units = [(s, gi) for s in seqs for gi in range(ngrp)]
    sheads = [(s, h) for s in seqs for h in range(heads)]

    def gcols(gi):
        return slice(gi * gw, (gi + 1) * gw)

    def hcols(h):
        return slice(h * dk, (h + 1) * dk)

    def chunk_body(c, carry):
        r0 = pl.multiple_of(c * chunk, chunk)
        rows = pl.ds(r0, chunk)
        qc = [qs[s, rows, :] for s in seqs]
        kc = [ks[s, rows, :] for s in seqs]
        bt = [bx[s, rows, :] for s in seqs]
        gcum = [_dot_exact_lhs(l_incl, gx[s, rows, :]) for s in seqs]
        gdiff = [_dot_exact_lhs(l_incl, jnp.where(strict, gpk[s, rows, :], 0.0)) for s in seqs]
        decay = [jnp.where(incl, jnp.exp(gdiff[s]), 0.0) for s in seqs]
        e_g = [jnp.exp(gcum[s]) for s in seqs]
        g_last = [gcum[s][chunk - 1:chunk, :] for s in seqs]
        e_last = [jnp.exp(g_last[s]) for s in seqs]
        kdec = [kc[s] * jnp.exp(g_last[s] - gcum[s]) for s in seqs]
        rhs_v = [bt[s] * vs[s, rows, :] for s in seqs]
        rhs_k = [bt[s] * e_g[s] * kc[s] for s in seqs]
        qe = [qc[s] * e_g[s] for s in seqs]

        kq = {}
        for s, gi in units:
            k_bf = kc[s][:, gcols(gi)].astype(BF16)
            lhs = jnp.concatenate([kc[s][:, gcols(gi)], qc[s][:, gcols(gi)]], axis=0).astype(BF16)
            kq[s, gi] = lax.dot_general(lhs, blockdiag(k_bf, m_nt), (((1,), (1,)), ((), ())),
                                        preferred_element_type=F32)
        dec_g = {(s, gi): decay[s][:, gi * pw:(gi + 1) * pw] for s, gi in units}
        bpack = [bpk[s, rows, :] for s in seqs]
        a_g = {(s, gi): jnp.where(strict_g, bpack[s][:, gi * pw:(gi + 1) * pw] * kq[s, gi][:chunk] * dec_g[s, gi],
                                  0.0) for s, gi in units}
        tinv = {u: eye - a_g[u] for u in units}
        xs = {u: _split2(a_g[u]) for u in units}
        xbd = {u: (blockdiag(xs[u][0], m_sq), blockdiag(xs[u][1], m_sq)) for u in units}
        for _ in range(n_sq):
            x = {u: dot3(xs[u][0], xs[u][1], xbd[u][0], xbd[u][1]) for u in units}
            xs = {u: _split2(x[u]) for u in units}
            xbd = {u: (blockdiag(xs[u][0], m_sq), blockdiag(xs[u][1], m_sq)) for u in units}
            ts = {u: _split2(tinv[u]) for u in units}
            tinv = {u: tinv[u] + dot3(ts[u][0], ts[u][1], xbd[u][0], xbd[u][1]) for u in units}
        uw = {}
        for s, gi in units:
            rh, rl = _split2(jnp.concatenate([rhs_v[s][:, gcols(gi)], rhs_k[s][:, gcols(gi)]], axis=1))
            th, tl = _split2(tinv[s, gi])
            uw[s, gi] = dot3(th, tl, blockdiag(rh, m_nt), blockdiag(rl, m_nt))
        s_all = {sh: s_scr[sh[0], sh[1]] for sh in sheads}
        wq = {}
        for s, h in sheads:
            gi, hl = divmod(h, hpg)
            w_h = uw[s, gi][:, gw + hl * dk:gw + (hl + 1) * dk]
            wq[s, h] = _bdot(jnp.concatenate([w_h, qe[s][:, hcols(h)]], axis=0), s_all[s, h])
        vn_bf = {}
        for s, gi in units:
            v_new = jnp.concatenate([uw[s, gi][:, hl * dk:(hl + 1) * dk] - wq[s, gi * hpg + hl][:chunk]
                                     for hl in range(hpg)], axis=1)
            vn_bf[s, gi] = v_new.astype(BF16)
        o_intra = {u: jnp.dot((kq[u][chunk:] * dec_g[u]).astype(BF16), blockdiag(vn_bf[u], m_nt),
                              preferred_element_type=F32) for u in units}
        upd = {}
        for s, h in sheads:
            gi, hl = divmod(h, hpg)
            upd[s, h] = _bdot_tn(kdec[s][:, hcols(h)], vn_bf[s, gi][:, hl * dk:(hl + 1) * dk])
        zc = [z_ref[0, s, rows, :] for s in seqs]
        for s, h in sheads:
            gi, hl = divmod(h, hpg)
            s_scr[s, h] = e_last[s][:, hcols(h)] * s_all[s, h] + upd[s, h]
            o_h = wq[s, h][chunk:] + o_intra[s, gi][:, hl * dk:(hl + 1) * dk]
            o_n = o_h * lax.rsqrt(jnp.mean(o_h * o_h, axis=-1, keepdims=True) + EPS) * gn
            o_ref[s, rows, hcols(h)] = o_n * _silu(zc[s][:, hcols(h)])
        return carry

    lax.fori_loop(0, tb // chunk, chunk_body, 0)
    s_out_ref[...] = s_scr[...]


def _gdn(p_all, ab, conv_w3, prm, gn, tail0, s0, *, n_seq, seq, ns, tb, chunk, seq_off, heads):
    planes, rows, hd = p_all.shape
    dk = hd // heads
    nblk = seq // tb
    conv_w = conv_w3.shape[0]
    pk = heads * chunk
    hpg = min(heads, MXU_DIM // chunk)
    assert seq % tb == 0 and tb % chunk == 0 and tb >= SUBLANES and chunk & (chunk - 1) == 0
    assert conv_w - 1 <= SUBLANES and pk % LANES == 0 and n_seq % ns == 0 and rows % seq == 0
    p4 = p_all.reshape(planes, rows // seq, seq, hd)
    ab3 = ab.reshape(rows // seq, seq, LANES)
    boff = seq_off // ns
    assert seq_off % ns == 0

    def plane(p):
        return pl.BlockSpec((1, ns, tb, hd), lambda b, t: (p, boff + b, t, 0))

    kern = functools.partial(_gdn_kernel, chunk=chunk, heads=heads, conv_w=conv_w)
    return pl.pallas_call(
        kern,
        out_shape=(
            jax.ShapeDtypeStruct((n_seq, seq, hd), F32),
            jax.ShapeDtypeStruct((n_seq, heads, dk, dk), F32),
            jax.ShapeDtypeStruct((n_seq, 3, SUBLANES, hd), F32),
        ),
        grid=(n_seq // ns, nblk),
        in_specs=[
            plane(0), plane(1), plane(2), plane(3),
            pl.BlockSpec((ns, tb, LANES), lambda b, t: (boff + b, t, 0)),
            pl.BlockSpec((conv_w, 3, hd), lambda b, t: (0, 0, 0)),
            pl.BlockSpec((SUBLANES, LANES), lambda b, t: (0, 0)),
            pl.BlockSpec((1, dk), lambda b, t: (0, 0)),
            pl.BlockSpec((3, SUBLANES, hd), lambda b, t: (0, 0, 0)),
            pl.BlockSpec((heads, dk, dk), lambda b, t: (0, 0, 0)),
        ],
        out_specs=(
            pl.BlockSpec((ns, tb, hd), lambda b, t: (b, t, 0)),
            pl.BlockSpec((ns, heads, dk, dk), lambda b, t: (b, 0, 0, 0)),
            pl.BlockSpec((ns, 3, SUBLANES, hd), lambda b, t: (b, 0, 0, 0)),
        ),
        scratch_shapes=[
            pltpu.VMEM((ns, 3, tb + SUBLANES, hd), F32),
            pltpu.VMEM((ns, tb, hd), F32), pltpu.VMEM((ns, tb, hd), F32), pltpu.VMEM((ns, tb, hd), F32),
            pltpu.VMEM((ns, tb, hd), F32), pltpu.VMEM((ns, tb, hd), F32),
            pltpu.VMEM((ns, tb, pk), F32), pltpu.VMEM((ns, tb, pk), F32),
            pltpu.VMEM((ns, heads, dk, dk), F32),
            pltpu.VMEM((hpg * chunk, hpg * chunk), BF16), pltpu.VMEM((hpg * chunk, 2 * hpg * dk), BF16),
        ],
        compiler_params=_cparams(("parallel", "arbitrary")),
        name=f"gdn_chunk{chunk}",
    )(p4, p4, p4, p4, ab3, conv_w3, prm, gn, tail0, s0)


def _sb_prompt_kernel(bias_ref, q_ref, k_ref, v_ref, o_ref, kbf, vtb, *, scale, tq, dh, n_meta):
    seq = k_ref.shape[1] - n_meta
    hps = k_ref.shape[2] // dh
    nkb = seq // tq
    hg = pl.program_id(1)
    qi = pl.program_id(2)
    hx = range(hps)

    def hsl(x):
        return slice(x * dh, (x + 1) * dh)

    @pl.when(qi == 0)
    def _():
        for x in hx:
            for j in range(nkb):
                rows = slice(n_meta + j * tq, n_meta + (j + 1) * tq)
                kbf[x, j] = k_ref[0, rows, hsl(x)].astype(BF16)
                vtb[x, j] = v_ref[0, rows, hsl(x)].T.astype(BF16)

    bias = [bias_ref[hg * hps + x] for x in hx]
    q_all = q_ref[0] * scale
    qt = [q_all[:, hsl(x)].T.astype(BF16) for x in hx]

    def suffix_mat(n):
        r = lax.broadcasted_iota(jnp.int32, (n, n), 0)
        c = lax.broadcasted_iota(jnp.int32, (n, n), 1)
        return jnp.where(c > r, 1.0, 0.0).astype(BF16)

    u_full = suffix_mat(tq)

    def logits(tiles):
        zt = [jnp.dot(k_bf, qt[x], preferred_element_type=F32) + bias[x] for x, k_bf, _, _ in tiles]
        sp = [_softplus(z) for z in zt]
        ls = [s if t[3] is None else jnp.where(t[3], s, 0.0) for s, t in zip(sp, tiles)]
        parts = [_split2(v) for v in ls]
        after = [jnp.dot(t[2], p[0], preferred_element_type=F32) for t, p in zip(tiles, parts)]
        after = [a + jnp.dot(t[2], p[1], preferred_element_type=F32) for a, t, p in zip(after, tiles, parts)]
        return [(t[0], z - s, l, a, t[3]) for z, s, l, a, t in zip(zt, sp, ls, after, tiles)]

    def attend(pre, vts, state):
        carry = list(state[:hps])
        acc = list(state[hps:])
        a_bf = []
        for (x, lz, ls, after, mask) in pre:
            a = jnp.exp(lz - after - carry[x])
            if mask is not None:
                a = jnp.where(mask, a, 0.0)
            a_bf.append(a.astype(BF16))
            carry[x] = carry[x] + after[0:1, :] + ls[0:1, :]
        for (x, _, _, _, _), a, vt_bf in zip(pre, a_bf, vts):
            acc[x] = acc[x] + jnp.dot(vt_bf, a, preferred_element_type=F32)
        return tuple(carry) + tuple(acc)

    r = lax.broadcasted_iota(jnp.int32, (tq, tq), 0)
    c = lax.broadcasted_iota(jnp.int32, (tq, tq), 1)
    u_meta = suffix_mat(n_meta)
    km = k_ref[0, 0:n_meta, :].astype(BF16)
    pre = logits([(x, kbf[x, qi], u_full, r < c) for x in hx] + [(x, km[:, hsl(x)], u_meta, None) for x in hx])
    pre_diag, pre_meta = pre[:hps], pre[hps:]
    state = tuple(jnp.zeros((1, tq), F32) for _ in hx) + tuple(jnp.zeros((dh, tq), F32) for _ in hx)
    state = attend(pre_diag, [vtb[x, qi] for x in hx], state)

    def pair(i, st):
        kb = qi - 1 - 2 * i
        tiles = [(x, kbf[x, kb - d], u_full, None) for d in range(2) for x in hx]
        return attend(logits(tiles), [vtb[x, kb - d] for d in range(2) for x in hx], st)

    state = lax.fori_loop(0, qi // 2, pair, state)

    def last_single(st):
        return attend(logits([(x, kbf[x, 0], u_full, None) for x in hx]), [vtb[x, 0] for x in hx], st)

    state = lax.cond(qi % 2 == 1, last_single, lambda st: st, state)
    vm = v_ref[0, 0:n_meta, :]
    state = attend(pre_meta, [vm[:, hsl(x)].T.astype(BF16) for x in hx], state)
    for x in hx:
        o_ref[:, hsl(x)] = state[hps + x].T


def _sb_prompt(p_all, k_full, v_full, bias, *, seq, heads, q_plane, tq, hps):
    hd = p_all.shape[2]
    dh = hd // heads
    nq = seq // tq
    n_seq, full_len, _ = k_full.shape
    n_meta = full_len - seq
    hw = hps * dh
    assert seq % tq == 0 and heads % hps == 0 and n_meta % SUBLANES == 0
    kern = functools.partial(_sb_prompt_kernel, scale=dh ** -0.5, tq=tq, dh=dh, n_meta=n_meta)
    return pl.pallas_call(
        kern,
        out_shape=jax.ShapeDtypeStruct((n_seq * seq, hd), F32),
        grid=(n_seq, heads // hps, nq),
        in_specs=[
            pl.BlockSpec(memory_space=pltpu.SMEM),
            pl.BlockSpec((1, tq, hw), lambda b, h, i: (q_plane, b * nq + i, h)),
            pl.BlockSpec((1, full_len, hw), lambda b, h, i: (b, 0, h)),
            pl.BlockSpec((1, full_len, hw), lambda b, h, i: (b, 0, h)),
        ],
        out_specs=pl.BlockSpec((tq, hw), lambda b, h, i: (b * nq + i, h)),
        scratch_shapes=[pltpu.VMEM((hps, nq, tq, dh), BF16), pltpu.VMEM((hps, nq, dh, tq), BF16)],
        compiler_params=_cparams(("parallel", "parallel", "arbitrary")),
        name="sb_prompt",
    )(bias, p_all, k_full, v_full)


def _merge_kernel(oa_ref, ob_ref, ga_ref, gb_ref, x_ref, wpa_ref, wpb_ref, wo_ref, out_ref):
    pa = jnp.dot(oa_ref[...].astype(BF16), wpa_ref[...], preferred_element_type=F32)
    pb = jnp.dot(ob_ref[...].astype(BF16), wpb_ref[...], preferred_element_type=F32)
    m = _sigmoid(ga_ref[0]) * pa + _sigmoid(gb_ref[0]) * pb
    out_ref[...] = x_ref[...] + jnp.dot(m.astype(BF16), wo_ref[...], preferred_element_type=F32)


def _merge(o_a, o_b, p_all, x, w_pa, w_pb, w_o, *, gate_plane, tm):
    m, d = x.shape
    da = o_a.shape[1]
    db = o_b.shape[1]
    assert m % tm == 0

    def whole(shape):
        return pl.BlockSpec(shape, lambda i: (0, 0))

    return pl.pallas_call(
        _merge_kernel,
        out_shape=jax.ShapeDtypeStruct((m, d), F32),
        grid=(m // tm,),
        in_specs=[
            pl.BlockSpec((tm, da), lambda i: (i, 0)),
            pl.BlockSpec((tm, db), lambda i: (i, 0)),
            pl.BlockSpec((1, tm, d), lambda i: (gate_plane, i, 0)),
            pl.BlockSpec((1, tm, d), lambda i: (gate_plane + 1, i, 0)),
            pl.BlockSpec((tm, d), lambda i: (i, 0)),
            whole(w_pa.shape), whole(w_pb.shape), whole(w_o.shape),
        ],
        out_specs=pl.BlockSpec((tm, d), lambda i: (i, 0)),
        compiler_params=_cparams(("parallel",)),
        name="merge",
    )(o_a, o_b, p_all, p_all, x, w_pa, w_pb, w_o)


def _ffn_kernel(h_ref, nw_ref, wg_ref, wu_ref, wo_ref, nf_ref, y_ref, xn_ref, acc_ref):
    f = pl.program_id(1)

    @pl.when(f == 0)
    def _():
        h = h_ref[...]
        xn_ref[...] = _rms(h, nw_ref[...]).astype(BF16)
        acc_ref[...] = h

    xn = xn_ref[...]
    gt = jnp.dot(xn, wg_ref[...], preferred_element_type=F32)
    up = jnp.dot(xn, wu_ref[...], preferred_element_type=F32)
    acc_ref[...] += jnp.dot((_silu(gt) * up).astype(BF16), wo_ref[...], preferred_element_type=F32)

    @pl.when(f == pl.num_programs(1) - 1)
    def _():
        y_ref[...] = _rms(acc_ref[...], nf_ref[...])


def _ffn(h, norm_w, w_in, w_out, norm_f, *, tm, tf):
    m, d = h.shape
    dff = w_out.shape[0]
    nf = dff // tf
    assert m % tm == 0 and dff % tf == 0
    return pl.pallas_call(
        _ffn_kernel,
        out_shape=jax.ShapeDtypeStruct((m, d), F32),
        grid=(m // tm, nf),
        in_specs=[
            pl.BlockSpec((tm, d), lambda i, f: (i, 0)),
            pl.BlockSpec((1, d), lambda i, f: (0, 0)),
            pl.BlockSpec((d, tf), lambda i, f: (0, f)),
            pl.BlockSpec((d, tf), lambda i, f: (0, f + nf)),
            pl.BlockSpec((tf, d), lambda i, f: (f, 0)),
            pl.BlockSpec((1, d), lambda i, f: (0, 0)),
        ],
        out_specs=pl.BlockSpec((tm, d), lambda i, f: (i, 0)),
        scratch_shapes=[pltpu.VMEM((tm, d), BF16), pltpu.VMEM((tm, d), F32)],
        compiler_params=_cparams(("parallel", "arbitrary")),
        name="ffn",
    )(h, norm_w, w_in, w_in, w_out, norm_f)


def _gdn_step_kernel(q_ref, k_ref, v_ref, z_ref, ab_ref, cw_ref, prm_ref, gn_ref, buf_ref, s_ref,
                     o_ref, s_out_ref, qs, ks, vs, gx, bx, *, heads, conv_w):
    nb = qs.shape[0]
    hd = qs.shape[1]
    dk = hd // heads
    b = pl.program_id(0)

    @pl.when(b == 0)
    def _():
        srcs = (q_ref, k_ref, v_ref)
        dsts = (qs, ks, vs)
        for p in range(3):
            y = cw_ref[conv_w - 1, p:p + 1, :] * srcs[p][0, 0:nb, :]
            for i in range(conv_w - 1):
                y = y + cw_ref[i, p:p + 1, :] * buf_ref[i, p]
            c = _silu(y)
            if p == 2:
                dsts[p][...] = c
            else:
                scale = dk ** -0.5 if p == 0 else 1.0
                for h in range(heads):
                    ch = c[:, h * dk:(h + 1) * dk]
                    inv = lax.rsqrt(jnp.sum(ch * ch, axis=-1, keepdims=True) + EPS)
                    dsts[p][:, h * dk:(h + 1) * dk] = ch * (inv * scale)
        ab = ab_ref[0:nb, :]
        g_t = -jnp.exp(prm_ref[0:1, :]) * _softplus(ab + prm_ref[1:2, :])
        b_t = _sigmoid(ab)
        gx[...] = _dot_exact_rhs(g_t, _head_select(hd, dk))
        bx[...] = _dot_exact_rhs(b_t, _head_select(hd, dk, heads))

    rb = pl.ds(b, 1)
    qrow = qs[rb, :]
    krow = ks[rb, :]
    vrow = vs[rb, :]
    e_g = jnp.exp(gx[rb, :])
    beta = bx[rb, :]
    zrow = z_ref[0, rb, :]
    gn = gn_ref[...]
    first = lax.broadcasted_iota(jnp.int32, (SUBLANES, dk), 0) == 0
    for h in range(heads):
        sl = slice(h * dk, (h + 1) * dk)
        s = s_ref[0, h]
        kh = krow[:, sl]
        qh = qrow[:, sl]
        lhs = jnp.where(first, kh, qh)
        prod = _dot_hi(jnp.broadcast_to(lhs, (SUBLANES, dk)), s)
        k_s = prod[0:1, :]
        q_s = prod[1:2, :]
        v_new = beta[:, sl] * (vrow[:, sl] - e_g[:, sl] * k_s)
        qk = jnp.sum(qh * kh, axis=-1, keepdims=True)
        o_h = e_g[:, sl] * q_s + qk * v_new
        k8 = jnp.where(first, kh, 0.0)
        v8 = jnp.where(first, v_new, 0.0)
        kh_hi, kh_lo = _split2(k8)
        vn_hi, vn_lo = _split2(v8)
        outer = (_bdot_tn(kh_hi, vn_hi) + _bdot_tn(kh_hi, vn_lo)) + _bdot_tn(kh_lo, vn_hi)
        s_out_ref[0, h] = e_g[:, sl] * s + outer
        o_n = o_h * lax.rsqrt(jnp.mean(o_h * o_h, axis=-1, keepdims=True) + EPS) * gn
        o_ref[0, :, sl] = o_n * _silu(zrow[:, sl])


def _gdn_step(p_small, ab, conv_w3, prm, gn, buf, state, *, heads):
    nb = state.shape[0]
    ms, hd = p_small.shape[1], p_small.shape[2]
    dk = hd // heads
    conv_w = conv_w3.shape[0]
    kern = functools.partial(_gdn_step_kernel, heads=heads, conv_w=conv_w)

    def plane(p):
        return pl.BlockSpec((1, ms, hd), lambda b: (p, 0, 0))

    return pl.pallas_call(
        kern,
        out_shape=(jax.ShapeDtypeStruct((nb, 1, hd), F32), jax.ShapeDtypeStruct(state.shape, F32)),
        grid=(nb,),
        in_specs=[
            plane(0), plane(1), plane(2), plane(3),
            pl.BlockSpec((ms, LANES), lambda b: (0, 0)),
            pl.BlockSpec((conv_w, 3, hd), lambda b: (0, 0, 0)),
            pl.BlockSpec((SUBLANES, LANES), lambda b: (0, 0)),
            pl.BlockSpec((1, dk), lambda b: (0, 0)),
            pl.BlockSpec((conv_w - 1, 3, nb, hd), lambda b: (0, 0, 0, 0)),
            pl.BlockSpec((1, heads, dk, dk), lambda b: (b, 0, 0, 0)),
        ],
        out_specs=(
            pl.BlockSpec((1, 1, hd), lambda b: (b, 0, 0)),
            pl.BlockSpec((1, heads, dk, dk), lambda b: (b, 0, 0, 0)),
        ),
        scratch_shapes=[pltpu.VMEM((nb, hd), F32) for _ in range(5)],
        compiler_params=_cparams(("arbitrary",)),
        name="gdn_step",
    )(p_small, p_small, p_small, p_small, ab, conv_w3, prm, gn, buf, state)


def _sb_step_kernel(pt_ref, q_ref, kn_ref, vn_ref, bias_ref, *refs, scale, pages_per_step, heads, past_len):
    k_refs = refs[:pages_per_step]
    v_refs = refs[pages_per_step:2 * pages_per_step]
    o_ref = refs[2 * pages_per_step]
    qbd, acc, carry = refs[2 * pages_per_step + 1:]
    dh = k_refs[0].shape[2]
    page = k_refs[0].shape[1] // heads
    hd = heads * dh
    b = pl.program_id(0)
    j = pl.program_id(1)
    lane_head = lax.broadcasted_iota(jnp.int32, (dh, LANES), 1)

    @pl.when(j == 0)
    def _():
        qrow = q_ref[pl.ds(b, 1), :] * scale
        for h in range(heads):
            qh_t = jnp.broadcast_to(qrow[:, h * dh:(h + 1) * dh], (LANES, dh)).T
            qbd[h * dh:(h + 1) * dh, :] = jnp.where(lane_head == h, qh_t, 0.0).astype(BF16)
        acc[...] = jnp.zeros_like(acc)
        carry[...] = jnp.zeros_like(carry)

    def head_major(ref):
        return jnp.concatenate([ref[0, pl.ds(h, page, stride=heads), :] for h in range(heads)], axis=1)

    r = lax.broadcasted_iota(jnp.int32, (page, page), 0)
    c = lax.broadcasted_iota(jnp.int32, (page, page), 1)
    u_bf = jnp.where(c > r, 1.0, 0.0).astype(BF16)
    expand = _head_select(hd, dh)
    bias = bias_ref[...]
    n = pages_per_step
    z = [jnp.dot(head_major(k_refs[i]).astype(BF16), qbd[...], preferred_element_type=F32) + bias
         for i in range(n)]
    sp = [_softplus(z[i]) for i in range(n)]
    parts = [_split2(sp[i]) for i in range(n)]
    after = [jnp.dot(u_bf, parts[i][0], preferred_element_type=F32) for i in range(n)]
    after = [after[i] + jnp.dot(u_bf, parts[i][1], preferred_element_type=F32) for i in range(n)]
    cur = carry[...]
    a = []
    for i in range(n):
        a.append(jnp.exp(z[i] - sp[i] - after[i] - cur).astype(BF16))
        cur = cur + after[i][0:1, :] + sp[i][0:1, :]
    carry[...] = cur
    a_x = [jnp.dot(a[i], expand, preferred_element_type=F32) for i in range(n)]
    contrib = a_x[0] * head_major(v_refs[0])
    for i in range(1, n):
        contrib = contrib + a_x[i] * head_major(v_refs[i])
    acc[...] += contrib

    @pl.when(j == pl.num_programs(1) - 1)
    def _():
        out = jnp.sum(acc[...], axis=0, keepdims=True)
        qrow = q_ref[pl.ds(b, 1), :] * scale
        krow = kn_ref[pl.ds(b, 1), :]
        vrow = vn_ref[pl.ds(b, 1), :]
        self_valid = past_len < past_len
        for h in range(heads):
            sl = slice(h * dh, (h + 1) * dh)
            z_new = jnp.sum(qrow[:, sl] * krow[:, sl], axis=-1, keepdims=True) + bias[:, h:h + 1]
            a_new = jnp.where(self_valid, jnp.exp(z_new - _softplus(z_new)), 0.0)
            o_ref[0, :, sl] = out[:, sl] + a_new * vrow[:, sl]


def _sb_step(page_table, q, k_new, v_new, bias_row, cache_k, cache_v, *, heads, pages_per_step):
    nb, n_pages = page_table.shape
    rows, dh = cache_k.shape[1], cache_k.shape[2]
    page = rows // heads
    hd = heads * dh
    nsteps = n_pages // pages_per_step
    assert n_pages % pages_per_step == 0
    kern = functools.partial(_sb_step_kernel, scale=dh ** -0.5, pages_per_step=pages_per_step,
                             heads=heads, past_len=n_pages * page)

    def page_spec(i):
        return pl.BlockSpec((1, rows, dh), lambda b, j, pt: (pt[b, n_pages - 1 - (j * pages_per_step + i)], 0, 0))

    def whole(shape):
        return pl.BlockSpec(shape, lambda b, j, pt: (0, 0))

    grid_spec = pltpu.PrefetchScalarGridSpec(
        num_scalar_prefetch=1,
        grid=(nb, nsteps),
        in_specs=[whole(q.shape), whole(k_new.shape), whole(v_new.shape), whole(bias_row.shape)]
        + [page_spec(i) for i in range(pages_per_step)] * 2,
        out_specs=pl.BlockSpec((1, 1, hd), lambda b, j, pt: (b, 0, 0)),
        scratch_shapes=[pltpu.VMEM((hd, LANES), BF16), pltpu.VMEM((page, hd), F32), pltpu.VMEM((1, LANES), F32)],
    )
    out = pl.pallas_call(
        kern,
        out_shape=jax.ShapeDtypeStruct((nb, 1, hd), F32),
        grid_spec=grid_spec,
        compiler_params=_cparams(("parallel", "arbitrary")),
        name="sb_step",
    )(page_table, q, k_new, v_new, bias_row, *([cache_k] * pages_per_step), *([cache_v] * pages_per_step))
    return out.reshape(nb, hd)


def _layer(l, x_prompt, x_sample, cache_k, cache_v, state_rec, state_conv, page_table, meta_tokens,
           norm_mix, w_in, conv_w, a_log, dt_bias, gdn_norm, sb_bias, w_pa, w_pb, w_o,
           norm_ffn, w_ffn_in, w_ffn_out, norm_final):
    n_seq, seq, d = x_prompt.shape
    nb = x_sample.shape[0]
    n_meta = meta_tokens.shape[0]
    heads_a = a_log.shape[1]
    heads_b = sb_bias.shape[1]
    d_qkv = conv_w.shape[2]
    d_a = d_qkv // 3
    d_b = w_pb.shape[1]
    conv_width = conv_w.shape[1]
    dk = d_a // heads_a
    assert d_a == d and d_b == d and 2 * heads_a <= LANES and nb % n_meta == 0

    w = w_in[l]
    c_ab = d_qkv + d_a
    c_qb = c_ab + 2 * heads_a
    c_gate = c_qb + 3 * d_b
    w_main = jnp.concatenate([w[:, :c_ab], w[:, c_qb:c_qb + d_b], w[:, c_gate:], w[:, c_qb + d_b:c_gate]],
                             axis=1).astype(BF16)
    w_ab = jnp.pad(w[:, c_ab:c_ab + 2 * heads_a], ((0, 0), (0, LANES - 2 * heads_a))).astype(BF16)
    nm = norm_mix[l].reshape(1, d)
    conv_w3 = conv_w[l].reshape(conv_width, 3, d_a)
    prm = jnp.zeros((SUBLANES, LANES), F32).at[0, :heads_a].set(a_log[l]).at[1, :heads_a].set(dt_bias[l])
    gn = gdn_norm[l].reshape(1, dk)
    bias = sb_bias[l]
    wpa, wpb, wo = w_pa[l].astype(BF16), w_pb[l].astype(BF16), w_o[l].astype(BF16)
    wfi, wfo = w_ffn_in[l].astype(BF16), w_ffn_out[l].astype(BF16)
    nffn = norm_ffn[l].reshape(1, d)
    nfin = norm_final.reshape(1, d)

    ms = -(-(nb + n_meta) // 64) * 64
    x_small = jnp.concatenate([x_sample.reshape(nb, d), meta_tokens.astype(F32),
                               jnp.zeros((ms - nb - n_meta, d), F32)], axis=0)
    p_small, ab_small, k_small, v_small = _inproj(x_small, nm, w_main, w_ab, tm=ms, seq=ms, row_off=0)
    q_plane, gate_plane = 4, 5

    zeros_tail = jnp.zeros((3, SUBLANES, d_a), F32)
    zeros_state = jnp.zeros((heads_a, dk, dk), F32)
    _, s_meta, tail_meta = _gdn(p_small, ab_small, conv_w3, prm, gn, zeros_tail, zeros_state,
                                n_seq=1, seq=n_meta, ns=1, tb=n_meta, chunk=n_meta,
                                seq_off=nb // n_meta, heads=heads_a)
    k_meta = k_small[0, nb:nb + n_meta]
    v_meta = v_small[0, nb:nb + n_meta]

    xp = x_prompt.reshape(n_seq * seq, d)
    p_all, ab_all, k_buf, v_buf = _inproj(xp, nm, w_main, w_ab, tm=1024, seq=seq, row_off=n_meta)
    k_full = _fill_rows(k_buf, k_meta)
    v_full = _fill_rows(v_buf, v_meta)
    o_a, s_prompt, tail_prompt = _gdn(p_all, ab_all, conv_w3, prm, gn, tail_meta[0], s_meta[0],
                                      n_seq=n_seq, seq=seq, ns=2 if n_seq % 2 == 0 else 1, tb=256, chunk=GDN_CHUNK,
                                      seq_off=0, heads=heads_a)
    o_a = o_a.reshape(n_seq * seq, d_a)
    o_b = _sb_prompt(p_all, k_full, v_full, bias, seq=seq, heads=heads_b, q_plane=q_plane, tq=256, hps=4)
    h1 = _merge(o_a, o_b, p_all, xp, wpa, wpb, wo, gate_plane=gate_plane, tm=512)
    y_prompt = _ffn(h1, nffn, wfi, wfo, nfin, tm=512, tf=1408).reshape(n_seq, seq, d)

    dh = d_b // heads_b
    k_prompt = k_full.reshape(n_seq, n_meta + seq, heads_b, dh)
    v_prompt = v_full.reshape(n_seq, n_meta + seq, heads_b, dh)
    conv_prompt = jnp.transpose(tail_prompt[:, :, SUBLANES - (conv_width - 1):, :], (0, 2, 1, 3)).reshape(
        n_seq, conv_width - 1, d_qkv)

    buf = jnp.transpose(state_conv[l].reshape(nb, conv_width - 1, 3, d_a), (1, 2, 0, 3))
    o_a_s, s_sample = _gdn_step(p_small, ab_small, conv_w3, prm, gn, buf, state_rec[l], heads=heads_a)
    o_a_s = o_a_s.reshape(nb, d_a)
    n_phys, page = cache_k.shape[1], cache_k.shape[2]
    bias_row = jnp.zeros((1, LANES), F32).at[0, :heads_b].set(bias)
    o_b_s = _sb_step(page_table, p_small[q_plane, :nb], k_small[0, :nb], v_small[0, :nb], bias_row,
                     cache_k[l].reshape(n_phys, page * heads_b, dh), cache_v[l].reshape(n_phys, page * heads_b, dh),
                     heads=heads_b, pages_per_step=8)
    h1_s = _merge(o_a_s, o_b_s, p_small, x_small[:nb], wpa, wpb, wo, gate_plane=gate_plane, tm=nb)
    y_sample = _ffn(h1_s, nffn, wfi, wfo, nfin, tm=nb, tf=1408).reshape(nb, 1, d)

    k_sample = k_small[0, :nb].reshape(nb, 1, heads_b, dh)
    v_sample = v_small[0, :nb].reshape(nb, 1, heads_b, dh)
    qkv_new = jnp.transpose(p_small[0:3, :nb], (1, 0, 2)).reshape(nb, 1, d_qkv)
    conv_sample = jnp.concatenate([state_conv[l][:, 1:], qkv_new], axis=1)
    return (y_prompt, y_sample, k_prompt, v_prompt, k_sample, v_sample,
            s_prompt, s_sample, conv_prompt, conv_sample)


def kernel(x_prompt, x_sample, cache_k, cache_v, state_rec, state_conv, page_table, meta_tokens, norm_mix, w_in, conv_w, a_log, dt_bias, gdn_norm, sb_bias, w_pa, w_pb, w_o, norm_ffn, w_ffn_in, w_ffn_out, norm_final):
    depth = w_in.shape[0]
    assert depth == 1 and x_sample.shape[1] == 1, "single layer, single decode token per sequence"
    outs = _layer(0, x_prompt, x_sample, cache_k, cache_v, state_rec, state_conv, page_table, meta_tokens,
                  norm_mix, w_in, conv_w, a_log, dt_bias, gdn_norm, sb_bias, w_pa, w_pb, w_o,
                  norm_ffn, w_ffn_in, w_ffn_out, norm_final)
    (y_prompt, y_sample, k_p, v_p, k_s, v_s, rec_p, rec_s, conv_p, conv_s) = outs
    return (y_prompt, y_sample, k_p[None], v_p[None], k_s[None], v_s[None],
            rec_p[None], rec_s[None], conv_p[None], conv_s[None])
```

```python
import functools

import jax
import jax.numpy as jnp
from jax import lax
from jax.experimental import pallas as pl
from jax.experimental.pallas import tpu as pltpu

F32 = jnp.float32
BF16 = jnp.bfloat16
EPS = 1e-6
LANES = 128
SUBLANES = 8
MXU_DIM = 256
GDN_CHUNK = 64
VMEM_LIMIT = 56 * 1024 * 1024


def _cparams(sem):
    return pltpu.CompilerParams(dimension_semantics=sem, vmem_limit_bytes=VMEM_LIMIT)


def _bdot(a, b):
    return jnp.dot(a.astype(BF16), b.astype(BF16), preferred_element_type=F32)


def _bdot_nt(a, b):
    return lax.dot_general(a.astype(BF16), b.astype(BF16), (((1,), (1,)), ((), ())),
                           preferred_element_type=F32)


def _bdot_tn(a, b):
    return lax.dot_general(a.astype(BF16), b.astype(BF16), (((0,), (0,)), ((), ())),
                           preferred_element_type=F32)


def _split2(x):
    hi = x.astype(BF16)
    lo = (x - hi.astype(F32)).astype(BF16)
    return hi, lo


def _split3(x):
    hi = x.astype(BF16)
    r = x - hi.astype(F32)
    mid = r.astype(BF16)
    lo = (r - mid.astype(F32)).astype(BF16)
    return hi, mid, lo


def _dot_exact_lhs(lhs_bf, x, passes=3):
    parts = _split3(x) if passes == 3 else _split2(x)
    out = jnp.dot(lhs_bf, parts[0], preferred_element_type=F32)
    for p in parts[1:]:
        out = out + jnp.dot(lhs_bf, p, preferred_element_type=F32)
    return out


def _dot_exact_rhs(x, rhs_bf, passes=3):
    parts = _split3(x) if passes == 3 else _split2(x)
    out = jnp.dot(parts[0], rhs_bf, preferred_element_type=F32)
    for p in parts[1:]:
        out = out + jnp.dot(p, rhs_bf, preferred_element_type=F32)
    return out


def _dot_hi(a, b):
    ah, al = _split2(a)
    bh, bl = _split2(b)
    out = jnp.dot(ah, bh, preferred_element_type=F32)
    out = out + jnp.dot(ah, bl, preferred_element_type=F32)
    return out + jnp.dot(al, bh, preferred_element_type=F32)


def _softplus(x):
    return jnp.maximum(x, 0.0) + jnp.log(1.0 + jnp.exp(-jnp.abs(x)))


def _sigmoid(x):
    return 1.0 / (1.0 + jnp.exp(-x))


def _silu(x):
    return x * _sigmoid(x)


def _rms(x, w):
    return x * lax.rsqrt(jnp.mean(x * x, axis=-1, keepdims=True) + EPS) * w


def _head_select(width, head_dim, lane_offset=0):
    shift = head_dim.bit_length() - 1
    assert 1 << shift == head_dim
    row = lax.broadcasted_iota(jnp.int32, (LANES, width), 0)
    col_head = lax.shift_right_logical(lax.broadcasted_iota(jnp.int32, (LANES, width), 1), shift)
    return jnp.where(row == col_head + lane_offset, 1.0, 0.0).astype(BF16)


def _inproj_kernel(x_ref, nw_ref, w_ref, wab_ref, out_ref, ab_ref, k_out_ref, v_out_ref, xn_ref):
    j = pl.program_id(1)
    n_planes = pl.num_programs(1) - 2

    @pl.when(j == 0)
    def _():
        xn_ref[...] = _rms(x_ref[...], nw_ref[...]).astype(BF16)
        ab_ref[...] = jnp.dot(xn_ref[...], wab_ref[...], preferred_element_type=F32)

    @pl.when(j < n_planes)
    def _():
        out_ref[0] = jnp.dot(xn_ref[...], w_ref[...], preferred_element_type=F32)

    @pl.when(j == n_planes)
    def _():
        k_out_ref[0] = jnp.dot(xn_ref[...], w_ref[...], preferred_element_type=F32)

    @pl.when(j == n_planes + 1)
    def _():
        v_out_ref[0] = jnp.dot(xn_ref[...], w_ref[...], preferred_element_type=F32)


def _inproj(x, norm_w, w_main, w_ab, tm, *, seq, row_off):
    m, d = x.shape
    n = w_main.shape[1]
    tn = d
    nj = n // tn
    tps = seq // tm
    assert m % seq == 0 and seq % tm == 0 and n % tn == 0 and row_off % SUBLANES == 0 and tm % SUBLANES == 0
    kv_shape = jax.ShapeDtypeStruct((m // seq, row_off + seq, tn), F32)
    kv_spec = pl.BlockSpec((pl.Element(1), pl.Element(tm), pl.Element(tn)),
                           lambda i, j: (i // tps, pl.multiple_of(row_off + (i % tps) * tm, SUBLANES), 0))
    return pl.pallas_call(
        _inproj_kernel,
        out_shape=(jax.ShapeDtypeStruct((nj - 2, m, tn), F32), jax.ShapeDtypeStruct((m, LANES), F32),
                   kv_shape, kv_shape),
        grid=(m // tm, nj),
        in_specs=[
            pl.BlockSpec((tm, d), lambda i, j: (i, 0)),
            pl.BlockSpec((1, d), lambda i, j: (0, 0)),
            pl.BlockSpec((d, tn), lambda i, j: (0, j)),
            pl.BlockSpec((d, LANES), lambda i, j: (0, 0)),
        ],
        out_specs=(
            pl.BlockSpec((1, tm, tn), lambda i, j: (jnp.minimum(j, nj - 3), i, 0)),
            pl.BlockSpec((tm, LANES), lambda i, j: (i, 0)),
            kv_spec, kv_spec,
        ),
        scratch_shapes=[pltpu.VMEM((tm, d), BF16)],
        compiler_params=_cparams(("parallel", "arbitrary")),
        name="inproj",
    )(x, norm_w, w_main, w_ab)


def _fill_rows_kernel(rows_ref, buf_ref, out_ref):
    del buf_ref
    out_ref[0] = rows_ref[...]


def _fill_rows(buf, rows):
    n, _, d = buf.shape
    r = rows.shape[0]
    assert r % SUBLANES == 0
    return pl.pallas_call(
        _fill_rows_kernel,
        out_shape=jax.ShapeDtypeStruct(buf.shape, buf.dtype),
        grid=(n,),
        in_specs=[pl.BlockSpec((r, d), lambda b: (0, 0)), pl.BlockSpec(memory_space=pl.ANY)],
        out_specs=pl.BlockSpec((1, r, d), lambda b: (b, 0, 0)),
        input_output_aliases={1: 0},
        compiler_params=_cparams(("parallel",)),
        name="fill_meta_rows",
    )(rows, buf)


def _gdn_kernel(q_ref, k_ref, v_ref, z_ref, ab_ref, cw_ref, prm_ref, gn_ref, tail0_ref, s0_ref,
                o_ref, s_out_ref, tail_out_ref,
                xpad, qs, ks, vs, gx, bx, gpk, bpk, s_scr, m_sq, m_nt, *, chunk, heads, conv_w):
    ns = q_ref.shape[1]
    tb = q_ref.shape[2]
    hd = q_ref.shape[3]
    dk = hd // heads
    pk = heads * chunk
    t = pl.program_id(1)

    @pl.when(t == 0)
    def _():
        for s in range(ns):
            s_scr[s] = s0_ref[...]
            xpad[s, :, 0:SUBLANES, :] = tail0_ref[...]

    srcs = (q_ref, k_ref, v_ref)
    dsts = (qs, ks, vs)
    sel_g, sel_b = _head_select(hd, dk), _head_select(hd, dk, heads)
    sel_gp, sel_bp = _head_select(pk, chunk), _head_select(pk, chunk, heads)
    for s in range(ns):
        for p in range(3):
            xpad[s, p, SUBLANES:SUBLANES + tb, :] = srcs[p][0, s]
        for p in range(3):
            y = None
            for i in range(conv_w):
                term = cw_ref[i, p:p + 1, :] * xpad[s, p, pl.ds(SUBLANES - (conv_w - 1) + i, tb), :]
                y = term if y is None else y + term
            c = _silu(y)
            if p == 2:
                dsts[p][s] = c
            else:
                scale = dk ** -0.5 if p == 0 else 1.0
                for h in range(heads):
                    ch = c[:, h * dk:(h + 1) * dk]
                    inv = lax.rsqrt(jnp.sum(ch * ch, axis=-1, keepdims=True) + EPS)
                    dsts[p][s, :, h * dk:(h + 1) * dk] = ch * (inv * scale)
        new_tail = xpad[s, :, tb:tb + SUBLANES, :]
        xpad[s, :, 0:SUBLANES, :] = new_tail
        tail_out_ref[s] = new_tail

        ab = ab_ref[s]
        g_t = -jnp.exp(prm_ref[0:1, :]) * _softplus(ab + prm_ref[1:2, :])
        b_t = _sigmoid(ab)
        gx[s] = _dot_exact_rhs(g_t, sel_g)
        bx[s] = _dot_exact_rhs(b_t, sel_b)
        gpk[s] = _dot_exact_rhs(g_t, sel_gp)
        bpk[s] = _dot_exact_rhs(b_t, sel_bp)

    hpg = min(heads, MXU_DIM // chunk)
    ngrp = heads // hpg
    pw = hpg * chunk
    gw = hpg * dk
    sh_c = chunk.bit_length() - 1
    sh_d = dk.bit_length() - 1

    def iota(shape, dim):
        return lax.broadcasted_iota(jnp.int32, shape, dim)

    ri = iota((chunk, pk), 0)
    ci = iota((chunk, pk), 1) & (chunk - 1)
    incl = ri >= ci
    strict = ri > ci
    eye = jnp.where(ri == ci, 1.0, 0.0)[:, :pw]
    strict_g = strict[:, :pw]
    l_incl = jnp.where(iota((chunk, chunk), 0) >= iota((chunk, chunk), 1), 1.0, 0.0).astype(BF16)
    @pl.when(t == 0)
    def _():
        m_sq[...] = jnp.where((iota((pw, pw), 0) >> sh_c) == (iota((pw, pw), 1) >> sh_c), 1.0, 0.0).astype(BF16)
        m_nt[...] = jnp.where((iota((pw, 2 * gw), 0) >> sh_c) == ((iota((pw, 2 * gw), 1) >> sh_d) & (hpg - 1)),
                              1.0, 0.0).astype(BF16)

    n_sq = sh_c - 1
    gn = gn_ref[...]

    def blockdiag(x_bf, mask_ref):
        return jnp.concatenate([x_bf] * hpg, axis=0) * mask_ref[:, :x_bf.shape[1]]

    def dot3(ah, al, bh, bl):
        out = jnp.dot(ah, bh, preferred_element_type=F32)
        out = out + jnp.dot(ah, bl, preferred_element_type=F32)
        return out + jnp.dot(al, bh, preferred_element_type=F32)

    seqs = range(ns)
    units = [(s, gi) for s in seqs for gi in range(ngrp)]
    sheads = [(s, h) for s in seqs for h in range(heads)]

    def gcols(gi):
        return slice(gi * gw, (gi + 1) * gw)

    def hcols(h):
        return slice(h * dk, (h + 1) * dk)

    def chunk_body(c, carry):
        r0 = pl.multiple_of(c * chunk, chunk)
        rows = pl.ds(r0, chunk)
        qc = [qs[s, rows, :] for s in seqs]
        kc = [ks[s, rows, :] for s in seqs]
        bt = [bx[s, rows, :] for s in seqs]
        gcum = [_dot_exact_lhs(l_incl, gx[s, rows, :]) for s in seqs]
        gdiff = [_dot_exact_lhs(l_incl, jnp.where(strict, gpk[s, rows, :], 0.0)) for s in seqs]
        decay = [jnp.where(incl, jnp.exp(gdiff[s]), 0.0) for s in seqs]
        e_g = [jnp.exp(gcum[s]) for s in seqs]
        g_last = [gcum[s][chunk - 1:chunk, :] for s in seqs]
        e_last = [jnp.exp(g_last[s]) for s in seqs]
        kdec = [kc[s] * jnp.exp(g_last[s] - gcum[s]) for s in seqs]
        rhs_v = [bt[s] * vs[s, rows, :] for s in seqs]
        rhs_k = [bt[s] * e_g[s] * kc[s] for s in seqs]
        qe = [qc[s] * e_g[s] for s in seqs]

        kq = {}
        for s, gi in units:
            k_bf = kc[s][:, gcols(gi)].astype(BF16)
            lhs = jnp.concatenate([kc[s][:, gcols(gi)], qc[s][:, gcols(gi)]], axis=0).astype(BF16)
            kq[s, gi] = lax.dot_general(lhs, blockdiag(k_bf, m_nt), (((1,), (1,)), ((), ())),
                                        preferred_element_type=F32)
        dec_g = {(s, gi): decay[s][:, gi * pw:(gi + 1) * pw] for s, gi in units}
        bpack = [bpk[s, rows, :] for s in seqs]
        a_g = {(s, gi): jnp.where(strict_g, bpack[s][:, gi * pw:(gi + 1) * pw] * kq[s, gi][:chunk] * dec_g[s, gi],
                                  0.0) for s, gi in units}
        tinv = {u: eye - a_g[u] for u in units}
        xs = {u: _split2(a_g[u]) for u in units}
        xbd = {u: (blockdiag(xs[u][0], m_sq), blockdiag(xs[u][1], m_sq)) for u in units}
        for _ in range(n_sq):
            x = {u: dot3(xs[u][0], xs[u][1], xbd[u][0], xbd[u][1]) for u in units}
            xs = {u: _split2(x[u]) for u in units}
            xbd = {u: (blockdiag(xs[u][0], m_sq), blockdiag(xs[u][1], m_sq)) for u in units}
            ts = {u: _split2(tinv[u]) for u in units}
            tinv = {u: tinv[u] + dot3(ts[u][0], ts[u][1], xbd[u][0], xbd[u][1]) for u in units}
        uw = {}
        for s, gi in units:
            rh, rl = _split2(jnp.concatenate([rhs_v[s][:, gcols(gi)], rhs_k[s][:, gcols(gi)]], axis=1))
            th, tl = _split2(tinv[s, gi])
            uw[s, gi] = dot3(th, tl, blockdiag(rh, m_nt), blockdiag(rl, m_nt))
        s_all = {sh: s_scr[sh[0], sh[1]] for sh in sheads}
        wq = {}
        for s, h in sheads:
            gi, hl = divmod(h, hpg)
            w_h = uw[s, gi][:, gw + hl * dk:gw + (hl + 1) * dk]
            wq[s, h] = _bdot(jnp.concatenate([w_h, qe[s][:, hcols(h)]], axis=0), s_all[s, h])
        vn_bf = {}
        for s, gi in units:
            v_new = jnp.concatenate([uw[s, gi][:, hl * dk:(hl + 1) * dk] - wq[s, gi * hpg + hl][:chunk]
                                     for hl in range(hpg)], axis=1)
            vn_bf[s, gi] = v_new.astype(BF16)
        o_intra = {u: jnp.dot((kq[u][chunk:] * dec_g[u]).astype(BF16), blockdiag(vn_bf[u], m_nt),
                              preferred_element_type=F32) for u in units}
        upd = {}
        for s, h in sheads:
            gi, hl = divmod(h, hpg)
            upd[s, h] = _bdot_tn(kdec[s][:, hcols(h)], vn_bf[s, gi][:, hl * dk:(hl + 1) * dk])
        zc = [z_ref[0, s, rows, :] for s in seqs]
        for s, h in sheads:
            gi, hl = divmod(h, hpg)
            s_scr[s, h] = e_last[s][:, hcols(h)] * s_all[s, h] + upd[s, h]
            o_h = wq[s, h][chunk:] + o_intra[s, gi][:, hl * dk:(hl + 1) * dk]
            o_n = o_h * lax.rsqrt(jnp.mean(o_h * o_h, axis=-1, keepdims=True) + EPS) * gn
            o_ref[s, rows, hcols(h)] = o_n * _silu(zc[s][:, hcols(h)])
        return carry

    lax.fori_loop(0, tb // chunk, chunk_body, 0)
    s_out_ref[...] = s_scr[...]


def _gdn(p_all, ab, conv_w3, prm, gn, tail0, s0, *, n_seq, seq, ns, tb, chunk, seq_off, heads):
    planes, rows, hd = p_all.shape
    dk = hd // heads
    nblk = seq // tb
    conv_w = conv_w3.shape[0]
    pk = heads * chunk
    hpg = min(heads, MXU_DIM // chunk)
    assert seq % tb == 0 and tb % chunk == 0 and tb >= SUBLANES and chunk & (chunk - 1) == 0
    assert conv_w - 1 <= SUBLANES and pk % LANES == 0 and n_seq % ns == 0 and rows % seq == 0
    p4 = p_all.reshape(planes, rows // seq, seq, hd)
    ab3 = ab.reshape(rows // seq, seq, LANES)
    boff = seq_off // ns
    assert seq_off % ns == 0

    def plane(p):
        return pl.BlockSpec((1, ns, tb, hd), lambda b, t: (p, boff + b, t, 0))

    kern = functools.partial(_gdn_kernel, chunk=chunk, heads=heads, conv_w=conv_w)
    return pl.pallas_call(
        kern,
        out_shape=(
            jax.ShapeDtypeStruct((n_seq, seq, hd), F32),
            jax.ShapeDtypeStruct((n_seq, heads, dk, dk), F32),
            jax.ShapeDtypeStruct((n_seq, 3, SUBLANES, hd), F32),
        ),
        grid=(n_seq // ns, nblk),
        in_specs=[
            plane(0), plane(1), plane(2), plane(3),
            pl.BlockSpec((ns, tb, LANES), lambda b, t: (boff + b, t, 0)),
            pl.BlockSpec((conv_w, 3, hd), lambda b, t: (0, 0, 0)),
            pl.BlockSpec((SUBLANES, LANES), lambda b, t: (0, 0)),
            pl.BlockSpec((1, dk), lambda b, t: (0, 0)),
            pl.BlockSpec((3, SUBLANES, hd), lambda b, t: (0, 0, 0)),
            pl.BlockSpec((heads, dk, dk), lambda b, t: (0, 0, 0)),
        ],
        out_specs=(
            pl.BlockSpec((ns, tb, hd), lambda b, t: (b, t, 0)),
            pl.BlockSpec((ns, heads, dk, dk), lambda b, t: (b, 0, 0, 0)),
            pl.BlockSpec((ns, 3, SUBLANES, hd), lambda b, t: (b, 0, 0, 0)),
        ),
        scratch_shapes=[
            pltpu.VMEM((ns, 3, tb + SUBLANES, hd), F32),
            pltpu.VMEM((ns, tb, hd), F32), pltpu.VMEM((ns, tb, hd), F32), pltpu.VMEM((ns, tb, hd), F32),
            pltpu.VMEM((ns, tb, hd), F32), pltpu.VMEM((ns, tb, hd), F32),
            pltpu.VMEM((ns, tb, pk), F32), pltpu.VMEM((ns, tb, pk), F32),
            pltpu.VMEM((ns, heads, dk, dk), F32),
            pltpu.VMEM((hpg * chunk, hpg * chunk), BF16), pltpu.VMEM((hpg * chunk, 2 * hpg * dk), BF16),
        ],
        compiler_params=_cparams(("parallel", "arbitrary")),
        name=f"gdn_chunk{chunk}",
    )(p4, p4, p4, p4, ab3, conv_w3, prm, gn, tail0, s0)


def _sb_prompt_kernel(bias_ref, q_ref, k_ref, v_ref, o_ref, kbf, vtb, *, scale, tq, dh, n_meta):
    seq = k_ref.shape[1] - n_meta
    hps = k_ref.shape[2] // dh
    nkb = seq // tq
    hg = pl.program_id(1)
    qi = pl.program_id(2)
    hx = range(hps)

    def hsl(x):
        return slice(x * dh, (x + 1) * dh)

    @pl.when(qi == 0)
    def _():
        for x in hx:
            for j in range(nkb):
                rows = slice(n_meta + j * tq, n_meta + (j + 1) * tq)
                kbf[x, j] = k_ref[0, rows, hsl(x)].astype(BF16)
                vtb[x, j] = v_ref[0, rows, hsl(x)].T.astype(BF16)

    bias = [bias_ref[hg * hps + x] for x in hx]
    q_all = q_ref[0] * scale
    qt = [q_all[:, hsl(x)].T.astype(BF16) for x in hx]

    def suffix_mat(n):
        r = lax.broadcasted_iota(jnp.int32, (n, n), 0)
        c = lax.broadcasted_iota(jnp.int32, (n, n), 1)
        return jnp.where(c > r, 1.0, 0.0).astype(BF16)

    u_full = suffix_mat(tq)

    def logits(tiles):
        zt = [jnp.dot(k_bf, qt[x], preferred_element_type=F32) + bias[x] for x, k_bf, _, _ in tiles]
        sp = [_softplus(z) for z in zt]
        ls = [s if t[3] is None else jnp.where(t[3], s, 0.0) for s, t in zip(sp, tiles)]
        parts = [_split2(v) for v in ls]
        after = [jnp.dot(t[2], p[0], preferred_element_type=F32) for t, p in zip(tiles, parts)]
        after = [a + jnp.dot(t[2], p[1], preferred_element_type=F32) for a, t, p in zip(after, tiles, parts)]
        return [(t[0], z - s, l, a, t[3]) for z, s, l, a, t in zip(zt, sp, ls, after, tiles)]

    def attend(pre, vts, state):
        carry = list(state[:hps])
        acc = list(state[hps:])
        a_bf = []
        for (x, lz, ls, after, mask) in pre:
            a = jnp.exp(lz - after - carry[x])
            if mask is not None:
                a = jnp.where(mask, a, 0.0)
            a_bf.append(a.astype(BF16))
            carry[x] = carry[x] + after[0:1, :] + ls[0:1, :]
        for (x, _, _, _, _), a, vt_bf in zip(pre, a_bf, vts):
            acc[x] = acc[x] + jnp.dot(vt_bf, a, preferred_element_type=F32)
        return tuple(carry) + tuple(acc)

    r = lax.broadcasted_iota(jnp.int32, (tq, tq), 0)
    c = lax.broadcasted_iota(jnp.int32, (tq, tq), 1)
    u_meta = suffix_mat(n_meta)
    km = k_ref[0, 0:n_meta, :].astype(BF16)
    pre = logits([(x, kbf[x, qi], u_full, r < c) for x in hx] + [(x, km[:, hsl(x)], u_meta, None) for x in hx])
    pre_diag, pre_meta = pre[:hps], pre[hps:]
    state = tuple(jnp.zeros((1, tq), F32) for _ in hx) + tuple(jnp.zeros((dh, tq), F32) for _ in hx)
    state = attend(pre_diag, [vtb[x, qi] for x in hx], state)

    def pair(i, st):
        kb = qi - 1 - 2 * i
        tiles = [(x, kbf[x, kb - d], u_full, None) for d in range(2) for x in hx]
        return attend(logits(tiles), [vtb[x, kb - d] for d in range(2) for x in hx], st)

    state = lax.fori_loop(0, qi // 2, pair, state)

    def last_single(st):
        return attend(logits([(x, kbf[x, 0], u_full, None) for x in hx]), [vtb[x, 0] for x in hx], st)

    state = lax.cond(qi % 2 == 1, last_single, lambda st: st, state)
    vm = v_ref[0, 0:n_meta, :]
    state = attend(pre_meta, [vm[:, hsl(x)].T.astype(BF16) for x in hx], state)
    for x in hx:
        o_ref[:, hsl(x)] = state[hps + x].T


def _sb_prompt(p_all, k_full, v_full, bias, *, seq, heads, q_plane, tq, hps):
    hd = p_all.shape[2]
    dh = hd // heads
    nq = seq // tq
    n_seq, full_len, _ = k_full.shape
    n_meta = full_len - seq
    hw = hps * dh
    assert seq % tq == 0 and heads % hps == 0 and n_meta % SUBLANES == 0
    kern = functools.partial(_sb_prompt_kernel, scale=dh ** -0.5, tq=tq, dh=dh, n_meta=n_meta)
    return pl.pallas_call(
        kern,
        out_shape=jax.ShapeDtypeStruct((n_seq * seq, hd), F32),
        grid=(n_seq, heads // hps, nq),
        in_specs=[
            pl.BlockSpec(memory_space=pltpu.SMEM),
            pl.BlockSpec((1, tq, hw), lambda b, h, i: (q_plane, b * nq + i, h)),
            pl.BlockSpec((1, full_len, hw), lambda b, h, i: (b, 0, h)),
            pl.BlockSpec((1, full_len, hw), lambda b, h, i: (b, 0, h)),
        ],
        out_specs=pl.BlockSpec((tq, hw), lambda b, h, i: (b * nq + i, h)),
        scratch_shapes=[pltpu.VMEM((hps, nq, tq, dh), BF16), pltpu.VMEM((hps, nq, dh, tq), BF16)],
        compiler_params=_cparams(("parallel", "parallel", "arbitrary")),
        name="sb_prompt",
    )(bias, p_all, k_full, v_full)


def _merge_kernel(oa_ref, ob_ref, ga_ref, gb_ref, x_ref, wpa_ref, wpb_ref, wo_ref, out_ref):
    pa = jnp.dot(oa_ref[...].astype(BF16), wpa_ref[...], preferred_element_type=F32)
    pb = jnp.dot(ob_ref[...].astype(BF16), wpb_ref[...], preferred_element_type=F32)
    m = _sigmoid(ga_ref[0]) * pa + _sigmoid(gb_ref[0]) * pb
    out_ref[...] = x_ref[...] + jnp.dot(m.astype(BF16), wo_ref[...], preferred_element_type=F32)


def _merge(o_a, o_b, p_all, x, w_pa, w_pb, w_o, *, gate_plane, tm):
    m, d = x.shape
    da = o_a.shape[1]
    db = o_b.shape[1]
    assert m % tm == 0

    def whole(shape):
        return pl.BlockSpec(shape, lambda i: (0, 0))

    return pl.pallas_call(
        _merge_kernel,
        out_shape=jax.ShapeDtypeStruct((m, d), F32),
        grid=(m // tm,),
        in_specs=[
            pl.BlockSpec((tm, da), lambda i: (i, 0)),
            pl.BlockSpec((tm, db), lambda i: (i, 0)),
            pl.BlockSpec((1, tm, d), lambda i: (gate_plane, i, 0)),
            pl.BlockSpec((1, tm, d), lambda i: (gate_plane + 1, i, 0)),
            pl.BlockSpec((tm, d), lambda i: (i, 0)),
            whole(w_pa.shape), whole(w_pb.shape), whole(w_o.shape),
        ],
        out_specs=pl.BlockSpec((tm, d), lambda i: (i, 0)),
        compiler_params=_cparams(("parallel",)),
        name="merge",
    )(o_a, o_b, p_all, p_all, x, w_pa, w_pb, w_o)


def _ffn_kernel(h_ref, nw_ref, wg_ref, wu_ref, wo_ref, nf_ref, y_ref, xn_ref, acc_ref):
    f = pl.program_id(1)

    @pl.when(f == 0)
    def _():
        h = h_ref[...]
        xn_ref[...] = _rms(h, nw_ref[...]).astype(BF16)
        acc_ref[...] = h

    xn = xn_ref[...]
    gt = jnp.dot(xn, wg_ref[...], preferred_element_type=F32)
    up = jnp.dot(xn, wu_ref[...], preferred_element_type=F32)
    acc_ref[...] += jnp.dot((_silu(gt) * up).astype(BF16), wo_ref[...], preferred_element_type=F32)

    @pl.when(f == pl.num_programs(1) - 1)
    def _():
        y_ref[...] = _rms(acc_ref[...], nf_ref[...])


def _ffn(h, norm_w, w_in, w_out, norm_f, *, tm, tf):
    m, d = h.shape
    dff = w_out.shape[0]
    nf = dff // tf
    assert m % tm == 0 and dff % tf == 0
    return pl.pallas_call(
        _ffn_kernel,
        out_shape=jax.ShapeDtypeStruct((m, d), F32),
        grid=(m // tm, nf),
        in_specs=[
            pl.BlockSpec((tm, d), lambda i, f: (i, 0)),
            pl.BlockSpec((1, d), lambda i, f: (0, 0)),
            pl.BlockSpec((d, tf), lambda i, f: (0, f)),
            pl.BlockSpec((d, tf), lambda i, f: (0, f + nf)),
            pl.BlockSpec((tf, d), lambda i, f: (f, 0)),
            pl.BlockSpec((1, d), lambda i, f: (0, 0)),
        ],
        out_specs=pl.BlockSpec((tm, d), lambda i, f: (i, 0)),
        scratch_shapes=[pltpu.VMEM((tm, d), BF16), pltpu.VMEM((tm, d), F32)],
        compiler_params=_cparams(("parallel", "arbitrary")),
        name="ffn",
    )(h, norm_w, w_in, w_in, w_out, norm_f)


def _gdn_step_kernel(q_ref, k_ref, v_ref, z_ref, ab_ref, cw_ref, prm_ref, gn_ref, buf_ref, s_ref,
                     o_ref, s_out_ref, qs, ks, vs, gx, bx, *, heads, conv_w):
    nb = qs.shape[0]
    hd = qs.shape[1]
    dk = hd // heads
    b = pl.program_id(0)

    @pl.when(b == 0)
    def _():
        srcs = (q_ref, k_ref, v_ref)
        dsts = (qs, ks, vs)
        for p in range(3):
            y = cw_ref[conv_w - 1, p:p + 1, :] * srcs[p][0, 0:nb, :]
            for i in range(conv_w - 1):
                y = y + cw_ref[i, p:p + 1, :] * buf_ref[i, p]
            c = _silu(y)
            if p == 2:
                dsts[p][...] = c
            else:
                scale = dk ** -0.5 if p == 0 else 1.0
                for h in range(heads):
                    ch = c[:, h * dk:(h + 1) * dk]
                    inv = lax.rsqrt(jnp.sum(ch * ch, axis=-1, keepdims=True) + EPS)
                    dsts[p][:, h * dk:(h + 1) * dk] = ch * (inv * scale)
        ab = ab_ref[0:nb, :]
        g_t = -jnp.exp(prm_ref[0:1, :]) * _softplus(ab + prm_ref[1:2, :])
        b_t = _sigmoid(ab)
        gx[...] = _dot_exact_rhs(g_t, _head_select(hd, dk))
        bx[...] = _dot_exact_rhs(b_t, _head_select(hd, dk, heads))

    rb = pl.ds(b, 1)
    qrow = qs[rb, :]
    krow = ks[rb, :]
    vrow = vs[rb, :]
    e_g = jnp.exp(gx[rb, :])
    beta = bx[rb, :]
    zrow = z_ref[0, rb, :]
    gn = gn_ref[...]
    first = lax.broadcasted_iota(jnp.int32, (SUBLANES, dk), 0) == 0
    for h in range(heads):
        sl = slice(h * dk, (h + 1) * dk)
        s = s_ref[0, h]
        kh = krow[:, sl]
        qh = qrow[:, sl]
        lhs = jnp.where(first, kh, qh)
        prod = _dot_hi(jnp.broadcast_to(lhs, (SUBLANES, dk)), s)
        k_s = prod[0:1, :]
        q_s = prod[1:2, :]
        v_new = beta[:, sl] * (vrow[:, sl] - e_g[:, sl] * k_s)
        qk = jnp.sum(qh * kh, axis=-1, keepdims=True)
        o_h = e_g[:, sl] * q_s + qk * v_new
        k8 = jnp.where(first, kh, 0.0)
        v8 = jnp.where(first, v_new, 0.0)
        kh_hi, kh_lo = _split2(k8)
        vn_hi, vn_lo = _split2(v8)
        outer = (_bdot_tn(kh_hi, vn_hi) + _bdot_tn(kh_hi, vn_lo)) + _bdot_tn(kh_lo, vn_hi)
        s_out_ref[0, h] = e_g[:, sl] * s + outer
        o_n = o_h * lax.rsqrt(jnp.mean(o_h * o_h, axis=-1, keepdims=True) + EPS) * gn
        o_ref[0, :, sl] = o_n * _silu(zrow[:, sl])


def _gdn_step(p_small, ab, conv_w3, prm, gn, buf, state, *, heads):
    nb = state.shape[0]
    ms, hd = p_small.shape[1], p_small.shape[2]
    dk = hd // heads
    conv_w = conv_w3.shape[0]
    kern = functools.partial(_gdn_step_kernel, heads=heads, conv_w=conv_w)

    def plane(p):
        return pl.BlockSpec((1, ms, hd), lambda b: (p, 0, 0))

    return pl.pallas_call(
        kern,
        out_shape=(jax.ShapeDtypeStruct((nb, 1, hd), F32), jax.ShapeDtypeStruct(state.shape, F32)),
        grid=(nb,),
        in_specs=[
            plane(0), plane(1), plane(2), plane(3),
            pl.BlockSpec((ms, LANES), lambda b: (0, 0)),
            pl.BlockSpec((conv_w, 3, hd), lambda b: (0, 0, 0)),
            pl.BlockSpec((SUBLANES, LANES), lambda b: (0, 0)),
            pl.BlockSpec((1, dk), lambda b: (0, 0)),
            pl.BlockSpec((conv_w - 1, 3, nb, hd), lambda b: (0, 0, 0, 0)),
            pl.BlockSpec((1, heads, dk, dk), lambda b: (b, 0, 0, 0)),
        ],
        out_specs=(
            pl.BlockSpec((1, 1, hd), lambda b: (b, 0, 0)),
            pl.BlockSpec((1, heads, dk, dk), lambda b: (b, 0, 0, 0)),
        ),
        scratch_shapes=[pltpu.VMEM((nb, hd), F32) for _ in range(5)],
        compiler_params=_cparams(("arbitrary",)),
        name="gdn_step",
    )(p_small, p_small, p_small, p_small, ab, conv_w3, prm, gn, buf, state)


def _sb_step_kernel(pt_ref, q_ref, kn_ref, vn_ref, bias_ref, *refs, scale, pages_per_step, heads, past_len):
    k_refs = refs[:pages_per_step]
    v_refs = refs[pages_per_step:2 * pages_per_step]
    o_ref = refs[2 * pages_per_step]
    qbd, acc, carry = refs[2 * pages_per_step + 1:]
    dh = k_refs[0].shape[2]
    page = k_refs[0].shape[1] // heads
    hd = heads * dh
    b = pl.program_id(0)
    j = pl.program_id(1)
    lane_head = lax.broadcasted_iota(jnp.int32, (dh, LANES), 1)

    @pl.when(j == 0)
    def _():
        qrow = q_ref[pl.ds(b, 1), :] * scale
        for h in range(heads):
            qh_t = jnp.broadcast_to(qrow[:, h * dh:(h + 1) * dh], (LANES, dh)).T
            qbd[h * dh:(h + 1) * dh, :] = jnp.where(lane_head == h, qh_t, 0.0).astype(BF16)
        acc[...] = jnp.zeros_like(acc)
        carry[...] = jnp.zeros_like(carry)

    def head_major(ref):
        return jnp.concatenate([ref[0, pl.ds(h, page, stride=heads), :] for h in range(heads)], axis=1)

    r = lax.broadcasted_iota(jnp.int32, (page, page), 0)
    c = lax.broadcasted_iota(jnp.int32, (page, page), 1)
    u_bf = jnp.where(c > r, 1.0, 0.0).astype(BF16)
    expand = _head_select(hd, dh)
    bias = bias_ref[...]
    n = pages_per_step
    z = [jnp.dot(head_major(k_refs[i]).astype(BF16), qbd[...], preferred_element_type=F32) + bias
         for i in range(n)]
    sp = [_softplus(z[i]) for i in range(n)]
    parts = [_split2(sp[i]) for i in range(n)]
    after = [jnp.dot(u_bf, parts[i][0], preferred_element_type=F32) for i in range(n)]
    after = [after[i] + jnp.dot(u_bf, parts[i][1], preferred_element_type=F32) for i in range(n)]
    cur = carry[...]
    a = []
    for i in range(n):
        a.append(jnp.exp(z[i] - sp[i] - after[i] - cur).astype(BF16))
        cur = cur + after[i][0:1, :] + sp[i][0:1, :]
    carry[...] = cur
    a_x = [jnp.dot(a[i], expand, preferred_element_type=F32) for i in range(n)]
    contrib = a_x[0] * head_major(v_refs[0])
    for i in range(1, n):
        contrib = contrib + a_x[i] * head_major(v_refs[i])
    acc[...] += contrib

    @pl.when(j == pl.num_programs(1) - 1)
    def _():
        out = jnp.sum(acc[...], axis=0, keepdims=True)
        qrow = q_ref[pl.ds(b, 1), :] * scale
        krow = kn_ref[pl.ds(b, 1), :]
        vrow = vn_ref[pl.ds(b, 1), :]
        self_valid = past_len < past_len
        for h in range(heads):
            sl = slice(h * dh, (h + 1) * dh)
            z_new = jnp.sum(qrow[:, sl] * krow[:, sl], axis=-1, keepdims=True) + bias[:, h:h + 1]
            a_new = jnp.where(self_valid, jnp.exp(z_new - _softplus(z_new)), 0.0)
            o_ref[0, :, sl] = out[:, sl] + a_new * vrow[:, sl]


def _sb_step(page_table, q, k_new, v_new, bias_row, cache_k, cache_v, *, heads, pages_per_step):
    nb, n_pages = page_table.shape
    rows, dh = cache_k.shape[1], cache_k.shape[2]
    page = rows // heads
    hd = heads * dh
    nsteps = n_pages // pages_per_step
    assert n_pages % pages_per_step == 0
    kern = functools.partial(_sb_step_kernel, scale=dh ** -0.5, pages_per_step=pages_per_step,
                             heads=heads, past_len=n_pages * page)

    def page_spec(i):
        return pl.BlockSpec((1, rows, dh), lambda b, j, pt: (pt[b, n_pages - 1 - (j * pages_per_step + i)], 0, 0))

    def whole(shape):
        return pl.BlockSpec(shape, lambda b, j, pt: (0, 0))

    grid_spec = pltpu.PrefetchScalarGridSpec(
        num_scalar_prefetch=1,
        grid=(nb, nsteps),
        in_specs=[whole(q.shape), whole(k_new.shape), whole(v_new.shape), whole(bias_row.shape)]
        + [page_spec(i) for i in range(pages_per_step)] * 2,
        out_specs=pl.BlockSpec((1, 1, hd), lambda b, j, pt: (b, 0, 0)),
        scratch_shapes=[pltpu.VMEM((hd, LANES), BF16), pltpu.VMEM((page, hd), F32), pltpu.VMEM((1, LANES), F32)],
    )
    out = pl.pallas_call(
        kern,
        out_shape=jax.ShapeDtypeStruct((nb, 1, hd), F32),
        grid_spec=grid_spec,
        compiler_params=_cparams(("parallel", "arbitrary")),
        name="sb_step",
    )(page_table, q, k_new, v_new, bias_row, *([cache_k] * pages_per_step), *([cache_v] * pages_per_step))
    return out.reshape(nb, hd)


def _layer(l, x_prompt, x_sample, cache_k, cache_v, state_rec, state_conv, page_table, meta_tokens,
           norm_mix, w_in, conv_w, a_log, dt_bias, gdn_norm, sb_bias, w_pa, w_pb, w_o,
           norm_ffn, w_ffn_in, w_ffn_out, norm_final):
    n_seq, seq, d = x_prompt.shape
    nb = x_sample.shape[0]
    n_meta = meta_tokens.shape[0]
    heads_a = a_log.shape[1]
    heads_b = sb_bias.shape[1]
    d_qkv = conv_w.shape[2]
    d_a = d_qkv // 3
    d_b = w_pb.shape[1]
    conv_width = conv_w.shape[1]
    dk = d_a // heads_a
    assert d_a == d and d_b == d and 2 * heads_a <= LANES and nb % n_meta == 0

    w = w_in[l]
    c_ab = d_qkv + d_a
    c_qb = c_ab + 2 * heads_a
    c_gate = c_qb + 3 * d_b
    w_main = jnp.concatenate([w[:, :c_ab], w[:, c_qb:c_qb + d_b], w[:, c_gate:], w[:, c_qb + d_b:c_gate]],
                             axis=1).astype(BF16)
    w_ab = jnp.pad(w[:, c_ab:c_ab + 2 * heads_a], ((0, 0), (0, LANES - 2 * heads_a))).astype(BF16)
    nm = norm_mix[l].reshape(1, d)
    conv_w3 = conv_w[l].reshape(conv_width, 3, d_a)
    prm = jnp.zeros((SUBLANES, LANES), F32).at[0, :heads_a].set(a_log[l]).at[1, :heads_a].set(dt_bias[l])
    gn = gdn_norm[l].reshape(1, dk)
    bias = sb_bias[l]
    wpa, wpb, wo = w_pa[l].astype(BF16), w_pb[l].astype(BF16), w_o[l].astype(BF16)
    wfi, wfo = w_ffn_in[l].astype(BF16), w_ffn_out[l].astype(BF16)
    nffn = norm_ffn[l].reshape(1, d)
    nfin = norm_final.reshape(1, d)

    ms = -(-(nb + n_meta) // 64) * 64
    x_small = jnp.concatenate([x_sample.reshape(nb, d), meta_tokens.astype(F32),
                               jnp.zeros((ms - nb - n_meta, d), F32)], axis=0)
    p_small, ab_small, k_small, v_small = _inproj(x_small, nm, w_main, w_ab, tm=ms, seq=ms, row_off=0)
    q_plane, gate_plane = 4, 5

    zeros_tail = jnp.zeros((3, SUBLANES, d_a), F32)
    zeros_state = jnp.zeros((heads_a, dk, dk), F32)
    _, s_meta, tail_meta = _gdn(p_small, ab_small, conv_w3, prm, gn, zeros_tail, zeros_state,
                                n_seq=1, seq=n_meta, ns=1, tb=n_meta, chunk=n_meta,
                                seq_off=nb // n_meta, heads=heads_a)
    k_meta = k_small[0, nb:nb + n_meta]
    v_meta = v_small[0, nb:nb + n_meta]

    xp = x_prompt.reshape(n_seq * seq, d)
    p_all, ab_all, k_buf, v_buf = _inproj(xp, nm, w_main, w_ab, tm=1024, seq=seq, row_off=n_meta)
    k_full = _fill_rows(k_buf, k_meta)
    v_full = _fill_rows(v_buf, v_meta)
    o_a, s_prompt, tail_prompt = _gdn(p_all, ab_all, conv_w3, prm, gn, tail_meta[0], s_meta[0],
                                      n_seq=n_seq, seq=seq, ns=2 if n_seq % 2 == 0 else 1, tb=256, chunk=GDN_CHUNK,
                                      seq_off=0, heads=heads_a)
    o_a = o_a.reshape(n_seq * seq, d_a)
    o_b = _sb_prompt(p_all, k_full, v_full, bias, seq=seq, heads=heads_b, q_plane=q_plane, tq=256, hps=4)
    h1 = _merge(o_a, o_b, p_all, xp, wpa, wpb, wo, gate_plane=gate_plane, tm=512)
    y_prompt = _ffn(h1, nffn, wfi, wfo, nfin, tm=512, tf=1408).reshape(n_seq, seq, d)

    dh = d_b // heads_b
    k_prompt = k_full.reshape(n_seq, n_meta + seq, heads_b, dh)
    v_prompt = v_full.reshape(n_seq, n_meta + seq, heads_b, dh)
    conv_prompt = jnp.transpose(tail_prompt[:, :, SUBLANES - (conv_width - 1):, :], (0, 2, 1, 3)).reshape(
        n_seq, conv_width - 1, d_qkv)

    buf = jnp.transpose(state_conv[l].reshape(nb, conv_width - 1, 3, d_a), (1, 2, 0, 3))
    o_a_s, s_sample = _gdn_step(p_small, ab_small, conv_w3, prm, gn, buf, state_rec[l], heads=heads_a)
    o_a_s = o_a_s.reshape(nb, d_a)
    n_phys, page = cache_k.shape[1], cache_k.shape[2]
    bias_row = jnp.zeros((1, LANES), F32).at[0, :heads_b].set(bias)
    o_b_s = _sb_step(page_table, p_small[q_plane, :nb], k_small[0, :nb], v_small[0, :nb], bias_row,
                     cache_k[l].reshape(n_phys, page * heads_b, dh), cache_v[l].reshape(n_phys, page * heads_b, dh),
                     heads=heads_b, pages_per_step=16)
    h1_s = _merge(o_a_s, o_b_s, p_small, x_small[:nb], wpa, wpb, wo, gate_plane=gate_plane, tm=nb)
    y_sample = _ffn(h1_s, nffn, wfi, wfo, nfin, tm=nb, tf=1408).reshape(nb, 1, d)

    k_sample = k_small[0, :nb].reshape(nb, 1, heads_b, dh)
    v_sample = v_small[0, :nb].reshape(nb, 1, heads_b, dh)
    qkv_new = jnp.transpose(p_small[0:3, :nb], (1, 0, 2)).reshape(nb, 1, d_qkv)
    conv_sample = jnp.concatenate([state_conv[l][:, 1:], qkv_new], axis=1)
    return (y_prompt, y_sample, k_prompt, v_prompt, k_sample, v_sample,
            s_prompt, s_sample, conv_prompt, conv_sample)


def kernel(x_prompt, x_sample, cache_k, cache_v, state_rec, state_conv, page_table, meta_tokens, norm_mix, w_in, conv_w, a_log, dt_bias, gdn_norm, sb_bias, w_pa, w_pb, w_o, norm_ffn, w_ffn_in, w_ffn_out, norm_final):
    depth = w_in.shape[0]
    assert depth == 1 and x_sample.shape[1] == 1, "single layer, single decode token per sequence"
    outs = _layer(0, x_prompt, x_sample, cache_k, cache_v, state_rec, state_conv, page_table, meta_tokens,
                  norm_mix, w_in, conv_w, a_log, dt_bias, gdn_norm, sb_bias, w_pa, w_pb, w_o,
                  norm_ffn, w_ffn_in, w_ffn_out, norm_final)
    (y_prompt, y_sample, k_p, v_p, k_s, v_s, rec_p, rec_s, conv_p, conv_s) = outs
    return (y_prompt, y_sample, k_p[None], v_p[None], k_s[None], v_s[None],
            rec_p[None], rec_s[None], conv_p[None], conv_s[None])
```
